```python
import math
import jax, jax.numpy as jnp
from jax import lax
import numpy as np

D_MODEL = 1024
BATCH = 8
SEQ = 2048
DEPTH = 4
DEC_BATCH = 128
DEC_SEQ = 8
PAST_LEN = 16384
PAGE_SIZE = 128

N_EVEN = (DEPTH + 1) // 2
N_ODD = DEPTH // 2
CONV_W = 4
NORM_EPS = 1e-6
RWKV_HEADS = 8
RWKV_HD = 64
RWKV_W = RWKV_HEADS * RWKV_HD
DECAY_LORA = 64
AAA_LORA = 64
GATE_LORA = 128
RWKV_PROJ = 3 * RWKV_W + DECAY_LORA + AAA_LORA + GATE_LORA
RWKV_GN_EPS = 64e-5
GDN_HEADS = 4
GDN_HD = 128
GDN_W = GDN_HEADS * GDN_HD
GDN_QKV = 3 * GDN_W
GDN_PROJ = GDN_QKV + GDN_W + 2 * GDN_HEADS
GDN_CHUNK = 64
EVEN_PROJ = RWKV_PROJ + GDN_PROJ
MIX_W = RWKV_W + GDN_W
LRU_W = D_MODEL
LRU_HEADS = 8
LRU_BW = LRU_W // LRU_HEADS
LRU_C = 8.0
D_FF = 4 * D_MODEL

kernel_name = 'rwkv7_gdn_rglru_hybrid_step'


def rmsnorm(x, w, eps=NORM_EPS):
    xf = x.astype(jnp.float32)
    y = xf * lax.rsqrt(jnp.mean(xf * xf, -1, keepdims=True) + eps)
    return (y * w.astype(jnp.float32)).astype(x.dtype)


def l2norm(x, eps=1e-6):
    xf = x.astype(jnp.float32)
    return xf * lax.rsqrt(jnp.sum(xf * xf, -1, keepdims=True) + eps)


def causal_conv(x, buf, w):
    T = x.shape[1]
    xp = jnp.concatenate([buf.astype(x.dtype), x], axis=1)
    y = sum(w[j] * xp[:, j:j + T] for j in range(CONV_W))
    return y, xp[:, T:]


def wkv7_scan(r, w, k, v, a, b, S0):
    def step(S, inp):
        r_t, w_t, k_t, v_t, a_t, b_t = inp
        sa = jnp.einsum('bhvk,bhk->bhv', S, a_t)
        S = S * w_t[:, :, None, :] + sa[..., None] * b_t[:, :, None, :] + v_t[..., None] * k_t[:, :, None, :]
        return S, jnp.einsum('bhvk,bhk->bhv', S, r_t)
    xs = tuple(jnp.swapaxes(t, 0, 1) for t in (r, w, k, v, a, b))
    S, y = lax.scan(step, S0, xs)
    return jnp.swapaxes(y, 0, 1), S


def rwkv7_mix(p, prev, mu, w0, w2, a0, a2, g2, k_k, k_a, r_k, ln_w, ln_b, S0):
    B, T, _ = p.shape
    f32 = jnp.float32
    shifted = jnp.concatenate([prev[:, None].astype(p.dtype), p[:, :-1]], axis=1)
    xs = p + mu * (shifted - p)
    o1, o2, o3 = RWKV_W, 2 * RWKV_W, 3 * RWKV_W
    o4 = o3 + DECAY_LORA
    o5 = o4 + AAA_LORA
    r, k, v = xs[..., :o1], xs[..., o1:o2], xs[..., o2:o3]
    xw, xa, xg = xs[..., o3:o4], xs[..., o4:o5], xs[..., o5:]
    w_log = -jax.nn.softplus(-(w0 + jnp.tanh(xw) @ w2).astype(f32)) - 0.5
    decay = jnp.exp(-jnp.exp(w_log))
    a = jax.nn.sigmoid((a0 + xa @ a2).astype(f32))
    g = jax.nn.sigmoid(xg) @ g2
    heads = lambda t: t.astype(f32).reshape(B, T, RWKV_HEADS, RWKV_HD)
    hk = lambda t: t.astype(f32).reshape(RWKV_HEADS, RWKV_HD)
    r, k, v, decay, a = heads(r), heads(k), heads(v), heads(decay), heads(a)
    kk = l2norm(k * hk(k_k))
    k = k * (1.0 + (a - 1.0) * hk(k_a))
    y, S = wkv7_scan(r, decay, k, v, -kk, kk * a, S0.astype(f32))
    mean = jnp.mean(y, -1, keepdims=True)
    var = jnp.mean(jnp.square(y - mean), -1, keepdims=True)
    y = (y - mean) * lax.rsqrt(var + RWKV_GN_EPS) * hk(ln_w) + hk(ln_b)
    y = y + jnp.sum(r * k * r_k.astype(f32), -1, keepdims=True) * v
    out = y.reshape(B, T, RWKV_W).astype(p.dtype) * g
    return out, S, p[:, -1]


def gated_delta_chunked(q, k, v, g, beta, S0):
    B, T, H, _ = q.shape
    C = min(GDN_CHUNK, T)
    pad = (-T) % C
    N = (T + pad) // C

    def chunks(t):
        t = jnp.pad(t, [(0, 0), (0, pad)] + [(0, 0)] * (t.ndim - 2))
        t = t.reshape((B, N, C) + t.shape[2:])
        perm = (1, 0, 3, 2, 4) if t.ndim == 5 else (1, 0, 3, 2)
        return t.transpose(perm)

    q, k, v, g, beta = chunks(q), chunks(k), chunks(v), chunks(g), chunks(beta)
    gc = jnp.cumsum(g, axis=-1)
    kb = k * beta[..., None]
    vb = v * beta[..., None]
    idx = jnp.arange(C)
    incl = idx[:, None] >= idx[None, :]
    strict = idx[:, None] > idx[None, :]
    diff = gc[..., :, None] - gc[..., None, :]
    decay = jnp.where(incl, jnp.exp(jnp.where(incl, diff, 0.0)), 0.0)
    L = jnp.where(strict, jnp.einsum('nbhid,nbhjd->nbhij', kb, k) * decay, 0.0)
    eye = jnp.broadcast_to(jnp.eye(C, dtype=L.dtype), L.shape)
    Tinv = lax.linalg.triangular_solve(eye + L, eye, left_side=True, lower=True, unit_diagonal=True)
    u = jnp.einsum('nbhij,nbhjd->nbhid', Tinv, vb)
    w = jnp.einsum('nbhij,nbhjd->nbhid', Tinv, kb * jnp.exp(gc)[..., None])
    qk = jnp.where(incl, jnp.einsum('nbhid,nbhjd->nbhij', q, k) * decay, 0.0)

    def step(S, inp):
        q_c, k_c, u_c, w_c, qk_c, gc_c = inp
        v_new = u_c - jnp.einsum('bhck,bhkv->bhcv', w_c, S)
        o = jnp.einsum('bhck,bhkv->bhcv', q_c * jnp.exp(gc_c)[..., None], S) + jnp.einsum('bhij,bhjv->bhiv', qk_c, v_new)
        g_last = gc_c[..., -1:]
        S = S * jnp.exp(g_last)[..., None] + jnp.einsum('bhck,bhcv->bhkv', k_c * jnp.exp(g_last - gc_c)[..., None], v_new)
        return S, o

    S, o = lax.scan(step, S0, (q, k, u, w, qk, gc))
    o = o.transpose(1, 0, 3, 2, 4).reshape(B, N * C, H, -1)[:, :T]
    return o, S


def gdn_mix(p, buf, conv_w, A_log, dt_bias, norm_w, S0):
    B, T, _ = p.shape
    f32 = jnp.float32
    qkv, new_buf = causal_conv(p[..., :GDN_QKV], buf, conv_w)
    qkv = jax.nn.silu(qkv)
    z = p[..., GDN_QKV:GDN_QKV + GDN_W]
    b_raw = p[..., GDN_QKV + GDN_W:GDN_QKV + GDN_W + GDN_HEADS]
    a_raw = p[..., GDN_QKV + GDN_W + GDN_HEADS:]
    heads = lambda t: t.reshape(B, T, GDN_HEADS, GDN_HD)
    q = l2norm(heads(qkv[..., :GDN_W])) * (GDN_HD ** -0.5)
    k = l2norm(heads(qkv[..., GDN_W:2 * GDN_W]))
    v = heads(qkv[..., 2 * GDN_W:]).astype(f32)
    beta = jax.nn.sigmoid(b_raw.astype(f32))
    g = -jnp.exp(A_log.astype(f32)) * jax.nn.softplus(a_raw.astype(f32) + dt_bias.astype(f32))
    o, S = gated_delta_chunked(q, k, v, g, beta, S0.astype(f32))
    o = o * lax.rsqrt(jnp.mean(o * o, -1, keepdims=True) + NORM_EPS) * norm_w.astype(f32)
    o = o * jax.nn.silu(heads(z).astype(f32))
    return o.reshape(B, T, GDN_W).astype(p.dtype), S, new_buf


def lru_mix(xn, w_in, conv_w, conv_b, wa, ba, wi, bi, L, h0, buf):
    B, T, _ = xn.shape
    f32 = jnp.float32
    proj = xn @ w_in
    gate_branch, xb = proj[..., :LRU_W], proj[..., LRU_W:]
    xc, new_buf = causal_conv(xb, buf, conv_w)
    xc = xc + conv_b
    xh = xc.reshape(B, T, LRU_HEADS, LRU_BW)
    r = jax.nn.sigmoid((jnp.einsum('bthi,hij->bthj', xh, wa) + ba).astype(f32))
    i = jax.nn.sigmoid((jnp.einsum('bthi,hij->bthj', xh, wi) + bi).astype(f32))
    log_a = -LRU_C * r * jax.nn.softplus(-L.astype(f32))
    mult = jnp.sqrt(-jnp.expm1(2.0 * log_a))
    a = jnp.exp(log_a).reshape(B, T, LRU_W)
    bvals = (mult * i * xh.astype(f32)).reshape(B, T, LRU_W)
    bvals = bvals.at[:, 0].add(a[:, 0] * h0.astype(f32))

    def comb(c1, c2):
        a1, b1 = c1
        a2, b2 = c2
        return a1 * a2, a2 * b1 + b2

    _, h = lax.associative_scan(comb, (a, bvals), axis=1)
    y = h.astype(xn.dtype) * jax.nn.gelu(gate_branch, approximate=True)
    return y, h[:, -1], new_buf


def sq_relu_mlp(x, up, down):
    return jnp.square(jax.nn.relu(x @ up)) @ down


def setup_inputs(seed: int = 0) -> dict:
    key = jax.random.key(seed)
    ks = jax.random.split(key, 48)
    f32 = jnp.float32

    def nrm(i, shape, scale):
        return jax.random.normal(ks[i], shape, f32) * scale

    def uni(i, shape, lo, hi):
        return jax.random.uniform(ks[i], shape, f32, lo, hi)

    dt = jnp.exp(uni(26, (N_EVEN, GDN_HEADS), math.log(1e-3), math.log(1e-1)))
    a_base = uni(36, (N_ODD, LRU_HEADS, LRU_BW), 0.9, 0.999)
    s = a_base ** (1.0 / LRU_C)
    return {
        'x_prompt': nrm(0, (BATCH, SEQ, D_MODEL), 1.0),
        'x_sample': nrm(1, (DEC_BATCH, DEC_SEQ, D_MODEL), 1.0),
        'state_rwkv': nrm(2, (N_EVEN, DEC_BATCH, RWKV_HEADS, RWKV_HD, RWKV_HD), 0.3),
        'state_rwkv_shift': nrm(3, (N_EVEN, DEC_BATCH, RWKV_PROJ), 1.0),
        'state_gdn': nrm(4, (N_EVEN, DEC_BATCH, GDN_HEADS, GDN_HD, GDN_HD), 0.3),
        'state_gdn_conv': nrm(5, (N_EVEN, DEC_BATCH, CONV_W - 1, GDN_QKV), 1.0),
        'state_lru': nrm(6, (N_ODD, DEC_BATCH, LRU_W), 0.5),
        'state_lru_conv': nrm(7, (N_ODD, DEC_BATCH, CONV_W - 1, LRU_W), 1.0),
        'norm_mix': 1.0 + nrm(8, (DEPTH, D_MODEL), 0.02),
        'norm_mlp': 1.0 + nrm(9, (DEPTH, D_MODEL), 0.02),
        'norm_final': 1.0 + nrm(10, (D_MODEL,), 0.02),
        'w_in_even': nrm(11, (N_EVEN, D_MODEL, EVEN_PROJ), D_MODEL ** -0.5),
        'w_out_even': nrm(12, (N_EVEN, MIX_W, D_MODEL), MIX_W ** -0.5),
        'rwkv_mu': uni(13, (N_EVEN, RWKV_PROJ), 0.0, 1.0),
        'rwkv_w0': uni(14, (N_EVEN, RWKV_W), -6.0, 0.0),
        'rwkv_w2': nrm(15, (N_EVEN, DECAY_LORA, RWKV_W), 0.5 * DECAY_LORA ** -0.5),
        'rwkv_a0': nrm(16, (N_EVEN, RWKV_W), 0.1),
        'rwkv_a2': nrm(17, (N_EVEN, AAA_LORA, RWKV_W), 0.5 * AAA_LORA ** -0.5),
        'rwkv_g2': nrm(18, (N_EVEN, GATE_LORA, RWKV_W), GATE_LORA ** -0.5),
        'rwkv_k_k': 0.85 + nrm(19, (N_EVEN, RWKV_W), 0.05),
        'rwkv_k_a': 1.0 + nrm(20, (N_EVEN, RWKV_W), 0.05),
        'rwkv_r_k': nrm(21, (N_EVEN, RWKV_HEADS, RWKV_HD), 0.1),
        'rwkv_ln_w': 1.0 + nrm(22, (N_EVEN, RWKV_W), 0.02),
        'rwkv_ln_b': nrm(23, (N_EVEN, RWKV_W), 0.02),
        'gdn_conv_w': nrm(24, (N_EVEN, CONV_W, GDN_QKV), CONV_W ** -0.5),
        'gdn_A_log': jnp.log(uni(25, (N_EVEN, GDN_HEADS), 1.0, 16.0)),
        'gdn_dt_bias': dt + jnp.log(-jnp.expm1(-dt)),
        'gdn_norm_w': 1.0 + nrm(27, (N_EVEN, GDN_HD), 0.02),
        'w_in_odd': nrm(28, (N_ODD, D_MODEL, 2 * LRU_W), D_MODEL ** -0.5),
        'w_out_odd': nrm(29, (N_ODD, LRU_W, D_MODEL), LRU_W ** -0.5),
        'lru_conv_w': nrm(30, (N_ODD, CONV_W, LRU_W), CONV_W ** -0.5),
        'lru_conv_b': nrm(31, (N_ODD, LRU_W), 0.02),
        'lru_wa': nrm(32, (N_ODD, LRU_HEADS, LRU_BW, LRU_BW), LRU_BW ** -0.5),
        'lru_ba': nrm(33, (N_ODD, LRU_HEADS, LRU_BW), 0.1),
        'lru_wi': nrm(34, (N_ODD, LRU_HEADS, LRU_BW, LRU_BW), LRU_BW ** -0.5),
        'lru_bi': nrm(35, (N_ODD, LRU_HEADS, LRU_BW), 0.1),
        'lru_L': jnp.log(s) - jnp.log1p(-s),
        'mlp_up': nrm(37, (DEPTH, D_MODEL, D_FF), D_MODEL ** -0.5),
        'mlp_down': nrm(38, (DEPTH, D_FF, D_MODEL), D_FF ** -0.5),
    }


def reference(x_prompt, x_sample, state_rwkv, state_rwkv_shift, state_gdn, state_gdn_conv, state_lru, state_lru_conv,
              norm_mix, norm_mlp, norm_final, w_in_even, w_out_even,
              rwkv_mu, rwkv_w0, rwkv_w2, rwkv_a0, rwkv_a2, rwkv_g2, rwkv_k_k, rwkv_k_a, rwkv_r_k, rwkv_ln_w, rwkv_ln_b,
              gdn_conv_w, gdn_A_log, gdn_dt_bias, gdn_norm_w,
              w_in_odd, w_out_odd, lru_conv_w, lru_conv_b, lru_wa, lru_ba, lru_wi, lru_bi, lru_L,
              mlp_up, mlp_down):

    def trunk(x, st_rwkv, st_shift, st_gdn, st_gconv, st_lru, st_lconv):
        h = x
        n_rwkv, n_shift, n_gdn, n_gconv, n_lru, n_lconv = [], [], [], [], [], []
        for l in range(DEPTH):
            i = l // 2
            xn = rmsnorm(h, norm_mix[l])
            if l % 2 == 0:
                proj = xn @ w_in_even[i]
                ya, S_a, sh = rwkv7_mix(proj[..., :RWKV_PROJ], st_shift[i], rwkv_mu[i], rwkv_w0[i], rwkv_w2[i],
                                        rwkv_a0[i], rwkv_a2[i], rwkv_g2[i], rwkv_k_k[i], rwkv_k_a[i], rwkv_r_k[i],
                                        rwkv_ln_w[i], rwkv_ln_b[i], st_rwkv[i])
                yb, S_b, cb = gdn_mix(proj[..., RWKV_PROJ:], st_gconv[i], gdn_conv_w[i], gdn_A_log[i],
                                      gdn_dt_bias[i], gdn_norm_w[i], st_gdn[i])
                h = h + jnp.concatenate([ya, yb], axis=-1) @ w_out_even[i]
                n_rwkv.append(S_a.astype(x.dtype))
                n_shift.append(sh)
                n_gdn.append(S_b.astype(x.dtype))
                n_gconv.append(cb)
            else:
                yc, hl, cl = lru_mix(xn, w_in_odd[i], lru_conv_w[i], lru_conv_b[i], lru_wa[i], lru_ba[i],
                                     lru_wi[i], lru_bi[i], lru_L[i], st_lru[i], st_lconv[i])
                h = h + yc @ w_out_odd[i]
                n_lru.append(hl.astype(x.dtype))
                n_lconv.append(cl)
            h = h + sq_relu_mlp(rmsnorm(h, norm_mlp[l]), mlp_up[l], mlp_down[l])
        y = rmsnorm(h, norm_final)
        return (y, jnp.stack(n_rwkv), jnp.stack(n_shift), jnp.stack(n_gdn), jnp.stack(n_gconv),
                jnp.stack(n_lru), jnp.stack(n_lconv))

    Bp = x_prompt.shape[0]
    dt = x_prompt.dtype
    y_prompt, rwkv_p, shift_p, gdn_p, gconv_p, lru_p, lconv_p = trunk(
        x_prompt,
        jnp.zeros((N_EVEN, Bp, RWKV_HEADS, RWKV_HD, RWKV_HD), dt),
        jnp.zeros((N_EVEN, Bp, RWKV_PROJ), dt),
        jnp.zeros((N_EVEN, Bp, GDN_HEADS, GDN_HD, GDN_HD), dt),
        jnp.zeros((N_EVEN, Bp, CONV_W - 1, GDN_QKV), dt),
        jnp.zeros((N_ODD, Bp, LRU_W), dt),
        jnp.zeros((N_ODD, Bp, CONV_W - 1, LRU_W), dt))
    y_sample, rwkv_s, shift_s, gdn_s, gconv_s, lru_s, lconv_s = trunk(
        x_sample, state_rwkv, state_rwkv_shift, state_gdn, state_gdn_conv, state_lru, state_lru_conv)
    return (y_prompt, y_sample, rwkv_p, rwkv_s, shift_p, shift_s, gdn_p, gdn_s, gconv_p, gconv_s, lru_p, lru_s, lconv_p, lconv_s)
```

```python
import functools

import jax
import jax.numpy as jnp
from jax import lax
from jax.experimental import pallas as pl
from jax.experimental.pallas import tpu as pltpu

F32 = jnp.float32
BF16 = jnp.bfloat16

D_MODEL = 1024
DEPTH = 4
CONV_W = 4
NORM_EPS = 1e-6
RWKV_HEADS = 8
RWKV_HD = 64
RWKV_W = RWKV_HEADS * RWKV_HD
DECAY_LORA = 64
AAA_LORA = 64
GATE_LORA = 128
RWKV_PROJ = 3 * RWKV_W + DECAY_LORA + AAA_LORA + GATE_LORA
RWKV_GN_EPS = 64e-5
GDN_HEADS = 4
GDN_HD = 128
GDN_W = GDN_HEADS * GDN_HD
GDN_QKV = 3 * GDN_W
GDN_BA_PAD = 128
LRU_W = D_MODEL
LRU_HEADS = 8
LRU_BW = LRU_W // LRU_HEADS
LRU_C = 8.0
D_FF = 4 * D_MODEL

SUBLANES = 8
VMEM_LIMIT_BYTES = 56 * 1024 * 1024


def _mm(a, b):
    return jnp.dot(a.astype(BF16), b.astype(BF16), preferred_element_type=F32)


def _mm_nt(a, b):
    return lax.dot_general(a.astype(BF16), b.astype(BF16), (((1,), (1,)), ((), ())),
                           preferred_element_type=F32)


def _mm_tn(a, b):
    return lax.dot_general(a.astype(BF16), b.astype(BF16), (((0,), (0,)), ((), ())),
                           preferred_element_type=F32)


def _split3(x):
    h1 = x.astype(BF16)
    r1 = x - h1.astype(F32)
    h2 = r1.astype(BF16)
    r2 = r1 - h2.astype(F32)
    return h1, h2, r2.astype(BF16)


def _mm_exact_lhs(m01, x):
    h1, h2, h3 = _split3(x)
    d = lambda t: jnp.dot(m01, t, preferred_element_type=F32)
    return d(h1) + d(h2) + d(h3)


def _mm_exact_rhs(x, m01):
    h1, h2, h3 = _split3(x)
    d = lambda t: jnp.dot(t, m01, preferred_element_type=F32)
    return d(h1) + d(h2) + d(h3)


def _mm3(a, b):
    ah = a.astype(BF16)
    al = (a - ah.astype(F32)).astype(BF16)
    bh = b.astype(BF16)
    bl = (b - bh.astype(F32)).astype(BF16)
    d = lambda x, y: jnp.dot(x, y, preferred_element_type=F32)
    return d(ah, bh) + d(ah, bl) + d(al, bh)


def _softplus(x):
    return jnp.maximum(x, 0.0) + jnp.log1p(jnp.exp(-jnp.abs(x)))


def _sigmoid(x):
    return jax.nn.sigmoid(x)


def _silu(x):
    return x * jax.nn.sigmoid(x)


def _gelu_tanh(x):
    return 0.5 * x * (1.0 + jnp.tanh(0.7978845608028654 * (x + 0.044715 * (x * x * x))))


def _tri_masks(c):
    row = lax.broadcasted_iota(jnp.int32, (c, c), 0)
    col = lax.broadcasted_iota(jnp.int32, (c, c), 1)
    return row >= col, row > col, row == col


def _unit_lower_inverse(m, eye, c):
    p = eye + m
    mk = m
    span = 2
    while span < c:
        mk = _mm3(mk, mk)
        p = p + _mm3(p, mk)
        span *= 2
    return p


def _const_spec(shape):
    nd = len(shape)
    return pl.BlockSpec(shape, lambda *_: (0,) * nd)


def _norm_proj_kernel(x_ref, nw_ref, w_ref, *o_refs, widths, col_chunk):
    x = x_ref[...]
    var = jnp.mean(x * x, axis=-1, keepdims=True)
    xn = (x * lax.rsqrt(var + NORM_EPS) * nw_ref[...]).astype(BF16)
    off = 0
    for o_ref, width in zip(o_refs, widths):
        for c0 in range(0, width, col_chunk):
            cw = min(col_chunk, width - c0)
            o_ref[:, c0:c0 + cw] = jnp.dot(xn, w_ref[:, off + c0:off + c0 + cw],
                                           preferred_element_type=F32)
        off += width


def _norm_proj(x, nw, w_bf16, widths, tm):
    n = x.shape[0]
    p = w_bf16.shape[1]
    assert sum(widths) == p and n % tm == 0
    kern = functools.partial(_norm_proj_kernel, widths=tuple(widths), col_chunk=512)
    return pl.pallas_call(
        kern,
        grid=(n // tm,),
        in_specs=[
            pl.BlockSpec((tm, D_MODEL), lambda i: (i, 0)),
            _const_spec((1, D_MODEL)),
            _const_spec((D_MODEL, p)),
        ],
        out_specs=[pl.BlockSpec((tm, wd), lambda i: (i, 0)) for wd in widths],
        out_shape=[jax.ShapeDtypeStruct((n, wd), F32) for wd in widths],
        compiler_params=pltpu.CompilerParams(
            dimension_semantics=("parallel",), vmem_limit_bytes=VMEM_LIMIT_BYTES),
        name="norm_proj",
    )(x, nw.reshape(1, D_MODEL), w_bf16)


def _mlp_kernel(*refs, n_mix, ff_chunk, final):
    h_ref = refs[0]
    y_refs = refs[1:1 + n_mix]
    wo_ref, nw_ref, up_ref, down_ref = refs[1 + n_mix:5 + n_mix]
    rest = refs[5 + n_mix:]
    if final:
        fnw_ref, o_ref, yf_ref = rest
    else:
        (o_ref,) = rest
    ys = [y_ref[...].astype(BF16) for y_ref in y_refs]
    y = ys[0] if n_mix == 1 else jnp.concatenate(ys, axis=1)
    h = h_ref[...] + jnp.dot(y, wo_ref[...], preferred_element_type=F32)
    var = jnp.mean(h * h, axis=-1, keepdims=True)
    xn = (h * lax.rsqrt(var + NORM_EPS) * nw_ref[...]).astype(BF16)
    acc = h
    for c0 in range(0, D_FF, ff_chunk):
        u = jnp.dot(xn, up_ref[:, c0:c0 + ff_chunk], preferred_element_type=F32)
        r = jnp.square(jnp.maximum(u, 0.0)).astype(BF16)
        acc = acc + jnp.dot(r, down_ref[c0:c0 + ff_chunk, :], preferred_element_type=F32)
    o_ref[...] = acc
    if final:
        var2 = jnp.mean(acc * acc, axis=-1, keepdims=True)
        yf_ref[...] = acc * lax.rsqrt(var2 + NORM_EPS) * fnw_ref[...]


def _mix_out_mlp(h, ys, wo, nw, up, down, final_nw, tm):
    n = h.shape[0]
    n_mix = len(ys)
    final = final_nw is not None
    kern = functools.partial(_mlp_kernel, n_mix=n_mix, ff_chunk=512, final=final)
    row = lambda wd: pl.BlockSpec((tm, wd), lambda i: (i, 0))
    in_specs = [row(D_MODEL)] + [row(y.shape[1]) for y in ys]
    in_specs += [_const_spec(wo.shape), _const_spec((1, D_MODEL)), _const_spec((D_MODEL, D_FF)),
                 _const_spec((D_FF, D_MODEL))]
    args = [h, *ys, wo, nw.reshape(1, D_MODEL), up, down]
    out_specs = [row(D_MODEL)]
    out_shape = [jax.ShapeDtypeStruct((n, D_MODEL), F32)]
    if final:
        in_specs.append(_const_spec((1, D_MODEL)))
        args.append(final_nw.reshape(1, D_MODEL))
        out_specs.append(row(D_MODEL))
        out_shape.append(jax.ShapeDtypeStruct((n, D_MODEL), F32))
    return pl.pallas_call(
        kern,
        grid=(n // tm,),
        in_specs=in_specs,
        out_specs=out_specs,
        out_shape=out_shape,
        compiler_params=pltpu.CompilerParams(
            dimension_semantics=("parallel",), vmem_limit_bytes=VMEM_LIMIT_BYTES),
        name="mix_out_mlp",
    )(*args)


def _rwkv_kernel(p_ref, prev_ref, s0_ref, mu_ref, w0_ref, w2_ref, a0_ref, a2_ref, g2_ref,
                 kk_ref, ka_ref, rk_ref, lnw_ref, lnb_ref, hsum_ref,
                 out_ref, s_out_ref, xp_ref, state_ref, y_ref, *, c, nc):
    ci = pl.program_id(1)
    base = SUBLANES

    @pl.when(ci == 0)
    def _():
        xp_ref[base - 1:base, :] = prev_ref[...]
        state_ref[...] = s0_ref[...]

    p = p_ref[...]
    xp_ref[base:base + c, :] = p
    shifted = xp_ref[base - 1:base - 1 + c, :]
    xp_ref[base - 1:base, :] = p[c - 1:c, :]

    xs = p + mu_ref[...] * (shifted - p)
    r = xs[:, 0:RWKV_W]
    k = xs[:, RWKV_W:2 * RWKV_W]
    v = xs[:, 2 * RWKV_W:3 * RWKV_W]
    xwa = xs[:, 3 * RWKV_W:3 * RWKV_W + DECAY_LORA + AAA_LORA]
    xg = xs[:, 3 * RWKV_W + DECAY_LORA + AAA_LORA:]

    w_log = -_softplus(-(w0_ref[...] + _mm(jnp.tanh(xwa), w2_ref[...]))) - 0.5
    ld = -jnp.exp(w_log)
    a = _sigmoid(a0_ref[...] + _mm(xwa, a2_ref[...]))
    g = _mm(_sigmoid(xg), g2_ref[...])

    incl, strict, diag = _tri_masks(c)
    eye = diag.astype(F32)
    cum = _mm_exact_lhs(incl.astype(BF16), ld)
    cum_last = cum[c - 1:c, :]
    w_in = jnp.exp(cum)
    w_ex = jnp.exp(cum - ld)
    w_inv = jnp.exp(-cum)
    w_tail = jnp.exp(cum_last - cum)
    w_last = jnp.exp(cum_last)

    hsum = hsum_ref[...]
    kkr = k * kk_ref[...]
    kk = kkr * lax.rsqrt(_mm_exact_rhs(kkr * kkr, hsum) + 1e-6)
    kp = k * (1.0 + (a - 1.0) * ka_ref[...])
    bv = kk * a
    a_t = -kk * w_ex
    b_t = bv * w_inv
    k_t = kp * w_inv
    r_t = r * w_in
    b_w = bv * w_tail
    k_w = kp * w_tail

    for h in range(RWKV_HEADS):
        sl = slice(h * RWKV_HD, (h + 1) * RWKV_HD)
        ar = jnp.concatenate([a_t[:, sl], r_t[:, sl]], axis=0)
        gb = _mm_nt(ar, b_t[:, sl])
        gk = _mm_nt(ar, k_t[:, sl])
        m_ab = jnp.where(strict, gb[:c], 0.0)
        m_ak = jnp.where(strict, gk[:c], 0.0)
        l_rb = jnp.where(incl, gb[c:], 0.0)
        l_rk = jnp.where(incl, gk[c:], 0.0)
        t_inv = _unit_lower_inverse(m_ab, eye, c)
        s0 = state_ref[h]
        vh = v[:, sl]
        s_proj = _mm_nt(ar, s0)
        u = _mm3(t_inv, s_proj[:c] + _mm(m_ak, vh))
        y_ref[:, sl] = s_proj[c:] + _mm(l_rb, u) + _mm(l_rk, vh)
        state_ref[h] = s0 * w_last[:, sl] + _mm_tn(u, b_w[:, sl]) + _mm_tn(vh, k_w[:, sl])

    y = y_ref[...]
    inv_hd = 1.0 / RWKV_HD
    mean = _mm_exact_rhs(y, hsum) * inv_hd
    d = y - mean
    var = _mm_exact_rhs(d * d, hsum) * inv_hd
    yn = d * lax.rsqrt(var + RWKV_GN_EPS) * lnw_ref[...] + lnb_ref[...]
    bonus = _mm_exact_rhs(r * kp * rk_ref[...], hsum)
    out_ref[...] = (yn + bonus * v) * g

    @pl.when(ci == nc - 1)
    def _():
        s_out_ref[...] = state_ref[...]


def _rwkv_mix(p, prev, s0, prm, c):
    b, t, _ = p.shape
    nc = t // c
    assert nc * c == t and c % SUBLANES == 0
    kern = functools.partial(_rwkv_kernel, c=c, nc=nc)
    consts = [prm["mu"], prm["w0"], prm["w2"], prm["a0"], prm["a2"], prm["g2"], prm["k_k"], prm["k_a"],
              prm["r_k"], prm["ln_w"], prm["ln_b"], prm["hsum"]]
    return pl.pallas_call(
        kern,
        grid=(b, nc),
        in_specs=[
            pl.BlockSpec((None, c, RWKV_PROJ), lambda i, j: (i, j, 0)),
            pl.BlockSpec((None, 1, RWKV_PROJ), lambda i, j: (i, 0, 0)),
            pl.BlockSpec((None, RWKV_HEADS, RWKV_HD, RWKV_HD), lambda i, j: (i, 0, 0, 0)),
        ] + [_const_spec(x.shape) for x in consts],
        out_specs=[
            pl.BlockSpec((None, c, RWKV_W), lambda i, j: (i, j, 0)),
            pl.BlockSpec((None, RWKV_HEADS, RWKV_HD, RWKV_HD), lambda i, j: (i, 0, 0, 0)),
        ],
        out_shape=[
            jax.ShapeDtypeStruct((b, t, RWKV_W), F32),
            jax.ShapeDtypeStruct((b, RWKV_HEADS, RWKV_HD, RWKV_HD), F32),
        ],
        scratch_shapes=[
            pltpu.VMEM((c + SUBLANES, RWKV_PROJ), F32),
            pltpu.VMEM((RWKV_HEADS, RWKV_HD, RWKV_HD), F32),
            pltpu.VMEM((c, RWKV_W), F32),
        ],
        compiler_params=pltpu.CompilerParams(
            dimension_semantics=("parallel", "arbitrary"), vmem_limit_bytes=VMEM_LIMIT_BYTES),
        name="rwkv7_mix",
    )(p, prev.reshape(b, 1, RWKV_PROJ), s0, *consts)


def _gdn_kernel(qkv_ref, z_ref, ba_ref, buf_ref, s0_ref, cw_ref, alog_ref, dt_ref, nw_ref,
                out_ref, s_out_ref, xp_ref, state_ref, *, c, nc):
    ci = pl.program_id(1)
    base = SUBLANES
    nb = CONV_W - 1

    @pl.when(ci == 0)
    def _():
        xp_ref[base - nb:base, :] = buf_ref[...]
        state_ref[...] = s0_ref[...]

    xp_ref[base:base + c, :] = qkv_ref[...]
    conv = cw_ref[0:1, :] * xp_ref[base - 3:base - 3 + c, :]
    for j in range(1, CONV_W):
        conv = conv + cw_ref[j:j + 1, :] * xp_ref[base - 3 + j:base - 3 + j + c, :]
    xp_ref[base - nb:base, :] = xp_ref[base + c - nb:base + c, :]
    qkv = _silu(conv)

    ba = ba_ref[...]
    beta_all = _sigmoid(ba)
    g_all = -jnp.exp(alog_ref[...]) * _softplus(ba + dt_ref[...])
    incl, strict, diag = _tri_masks(c)
    eye = diag.astype(F32)
    gc_all = _mm_exact_lhs(incl.astype(BF16), g_all)
    upper = jnp.logical_or(jnp.logical_not(incl), diag).astype(BF16)
    dn = (((0,), (0,)), ((), ()))
    gct = sum(lax.dot_general(t, upper, dn, preferred_element_type=F32) for t in _split3(g_all))

    for h in range(GDN_HEADS):
        sl = slice(h * GDN_HD, (h + 1) * GDN_HD)
        q = qkv[:, h * GDN_HD:(h + 1) * GDN_HD]
        k = qkv[:, GDN_W + h * GDN_HD:GDN_W + (h + 1) * GDN_HD]
        v = qkv[:, 2 * GDN_W + h * GDN_HD:2 * GDN_W + (h + 1) * GDN_HD]
        q = q * lax.rsqrt(jnp.sum(q * q, axis=-1, keepdims=True) + 1e-6) * (GDN_HD ** -0.5)
        k = k * lax.rsqrt(jnp.sum(k * k, axis=-1, keepdims=True) + 1e-6)
        beta = beta_all[:, h:h + 1]
        gcol = gc_all[:, GDN_HEADS + h:GDN_HEADS + h + 1]
        grow = gct[GDN_HEADS + h:GDN_HEADS + h + 1, :]
        decay = jnp.where(incl, jnp.exp(jnp.where(incl, gcol - grow, 0.0)), 0.0)
        kb = k * beta
        vb = v * beta
        gq = _mm_nt(jnp.concatenate([q, kb], axis=0), k)
        qk = jnp.where(incl, gq[:c] * decay, 0.0)
        lmat = jnp.where(strict, gq[c:] * decay, 0.0)
        t_inv = _unit_lower_inverse(-lmat, eye, c)
        eg = jnp.exp(gcol)
        uw = _mm3(t_inv, jnp.concatenate([vb, kb * eg], axis=1))
        s0 = state_ref[h]
        qw_s = _mm(jnp.concatenate([q * eg, uw[:, GDN_HD:]], axis=0), s0)
        v_new = uw[:, :GDN_HD] - qw_s[c:]
        o = qw_s[:c] + _mm(qk, v_new)
        g_last = gcol[c - 1:c, :]
        state_ref[h] = s0 * jnp.exp(g_last) + _mm_tn(k * jnp.exp(g_last - gcol), v_new)
        o = o * lax.rsqrt(jnp.mean(o * o, axis=-1, keepdims=True) + NORM_EPS) * nw_ref[...]
        out_ref[:, sl] = o * _silu(z_ref[:, sl])

    @pl.when(ci == nc - 1)
    def _():
        s_out_ref[...] = state_ref[...]


def _gdn_mix(qkv, z, ba, buf, s0, prm, c):
    b, t, _ = qkv.shape
    nc = t // c
    assert nc * c == t and c % SUBLANES == 0
    kern = functools.partial(_gdn_kernel, c=c, nc=nc)
    consts = [prm["conv_w"], prm["a_log"], prm["dt_bias"], prm["norm_w"]]
    return pl.pallas_call(
        kern,
        grid=(b, nc),
        in_specs=[
            pl.BlockSpec((None, c, GDN_QKV), lambda i, j: (i, j, 0)),
            pl.BlockSpec((None, c, GDN_W), lambda i, j: (i, j, 0)),
            pl.BlockSpec((None, c, GDN_BA_PAD), lambda i, j: (i, j, 0)),
            pl.BlockSpec((None, CONV_W - 1, GDN_QKV), lambda i, j: (i, 0, 0)),
            pl.BlockSpec((None, GDN_HEADS, GDN_HD, GDN_HD), lambda i, j: (i, 0, 0, 0)),
        ] + [_const_spec(x.shape) for x in consts],
        out_specs=[
            pl.BlockSpec((None, c, GDN_W), lambda i, j: (i, j, 0)),
            pl.BlockSpec((None, GDN_HEADS, GDN_HD, GDN_HD), lambda i, j: (i, 0, 0, 0)),
        ],
        out_shape=[
            jax.ShapeDtypeStruct((b, t, GDN_W), F32),
            jax.ShapeDtypeStruct((b, GDN_HEADS, GDN_HD, GDN_HD), F32),
        ],
        scratch_shapes=[
            pltpu.VMEM((c + SUBLANES, GDN_QKV), F32),
            pltpu.VMEM((GDN_HEADS, GDN_HD, GDN_HD), F32),
        ],
        compiler_params=pltpu.CompilerParams(
            dimension_semantics=("parallel", "arbitrary"), vmem_limit_bytes=VMEM_LIMIT_BYTES),
        name="gdn_mix",
    )(qkv, z, ba, buf, s0, *consts)


def _lru_kernel(gate_ref, xb_ref, buf_ref, h0_ref, cw_ref, cb_ref, wab_ref, bab_ref, l_ref,
                out_ref, h_out_ref, xp_ref, carry_ref, sa_ref, sb_ref, *, c, nc):
    ci = pl.program_id(1)
    base = SUBLANES
    nb = CONV_W - 1

    @pl.when(ci == 0)
    def _():
        xp_ref[base - nb:base, :] = buf_ref[...]
        carry_ref[...] = h0_ref[...]
        sa_ref[0:c, :] = jnp.ones((c, LRU_W), F32)
        sb_ref[0:c, :] = jnp.zeros((c, LRU_W), F32)

    xp_ref[base:base + c, :] = xb_ref[...]
    xc = cw_ref[0:1, :] * xp_ref[base - 3:base - 3 + c, :]
    for j in range(1, CONV_W):
        xc = xc + cw_ref[j:j + 1, :] * xp_ref[base - 3 + j:base - 3 + j + c, :]
    xp_ref[base - nb:base, :] = xp_ref[base + c - nb:base + c, :]
    xc = xc + cb_ref[...]

    for h in range(LRU_HEADS):
        sl = slice(h * LRU_BW, (h + 1) * LRU_BW)
        xh = xc[:, sl]
        ri = _mm(xh, wab_ref[h]) + bab_ref[h]
        r = _sigmoid(ri[:, :LRU_BW])
        i = _sigmoid(ri[:, LRU_BW:])
        log_a = -LRU_C * r * _softplus(-l_ref[:, sl])
        a = jnp.exp(log_a)
        mult = jnp.sqrt(-jnp.tanh(log_a) * (a * a + 1.0))
        sa_ref[c:2 * c, sl] = a
        sb_ref[c:2 * c, sl] = mult * i * xh

    a = sa_ref[c:2 * c, :]
    b = sb_ref[c:2 * c, :]
    d = 1
    while d < c:
        a_sh = sa_ref[c - d:2 * c - d, :]
        b_sh = sb_ref[c - d:2 * c - d, :]
        b = a * b_sh + b
        a = a * a_sh
        d *= 2
        if d < c:
            sa_ref[c:2 * c, :] = a
            sb_ref[c:2 * c, :] = b
    hs = a * carry_ref[...] + b
    carry_ref[...] = hs[c - 1:c, :]
    out_ref[...] = hs * _gelu_tanh(gate_ref[...])

    @pl.when(ci == nc - 1)
    def _():
        h_out_ref[...] = hs[c - 1:c, :]


def _lru_mix(gate, xb, buf, h0, prm, c):
    b, t, _ = xb.shape
    nc = t // c
    assert nc * c == t and c % SUBLANES == 0
    kern = functools.partial(_lru_kernel, c=c, nc=nc)
    consts = [prm["conv_w"], prm["conv_b"], prm["wab"], prm["bab"], prm["l"]]
    y, h_last = pl.pallas_call(
        kern,
        grid=(b, nc),
        in_specs=[
            pl.BlockSpec((None, c, LRU_W), lambda i, j: (i, j, 0)),
            pl.BlockSpec((None, c, LRU_W), lambda i, j: (i, j, 0)),
            pl.BlockSpec((None, CONV_W - 1, LRU_W), lambda i, j: (i, 0, 0)),
            pl.BlockSpec((None, 1, LRU_W), lambda i, j: (i, 0, 0)),
        ] + [_const_spec(x.shape) for x in consts],
        out_specs=[
            pl.BlockSpec((None, c, LRU_W), lambda i, j: (i, j, 0)),
            pl.BlockSpec((None, 1, LRU_W), lambda i, j: (i, 0, 0)),
        ],
        out_shape=[
            jax.ShapeDtypeStruct((b, t, LRU_W), F32),
            jax.ShapeDtypeStruct((b, 1, LRU_W), F32),
        ],
        scratch_shapes=[
            pltpu.VMEM((c + SUBLANES, LRU_W), F32),
            pltpu.VMEM((1, LRU_W), F32),
            pltpu.VMEM((2 * c, LRU_W), F32),
            pltpu.VMEM((2 * c, LRU_W), F32),
        ],
        compiler_params=pltpu.CompilerParams(
            dimension_semantics=("parallel", "arbitrary"), vmem_limit_bytes=VMEM_LIMIT_BYTES),
        name="lru_mix",
    )(gate, xb, buf, h0.reshape(b, 1, LRU_W), *consts)
    return y, h_last.reshape(b, LRU_W)


def _prep_even(i, w_in_even, w_out_even, rwkv_mu, rwkv_w0, rwkv_w2, rwkv_a0, rwkv_a2, rwkv_g2, rwkv_k_k,
               rwkv_k_a, rwkv_r_k, rwkv_ln_w, rwkv_ln_b, gdn_conv_w, gdn_A_log, gdn_dt_bias, gdn_norm_w):
    w = w_in_even[i]
    o = RWKV_PROJ
    n_ba = 2 * GDN_HEADS
    w_ba = jnp.pad(w[:, o + GDN_QKV + GDN_W:], ((0, 0), (0, GDN_BA_PAD - n_ba)))
    w_cat = jnp.concatenate([w[:, :o + GDN_QKV + GDN_W], w_ba], axis=1).astype(BF16)
    row = lambda x: x.reshape(1, -1)
    lane = jnp.arange(RWKV_W) // RWKV_HD
    lora = DECAY_LORA + AAA_LORA
    rwkv = dict(
        mu=row(rwkv_mu[i]), w0=row(rwkv_w0[i]), a0=row(rwkv_a0[i]),
        w2=jnp.pad(rwkv_w2[i], ((0, lora - DECAY_LORA), (0, 0))).astype(BF16),
        a2=jnp.pad(rwkv_a2[i], ((DECAY_LORA, 0), (0, 0))).astype(BF16),
        g2=rwkv_g2[i].astype(BF16),
        k_k=row(rwkv_k_k[i]), k_a=row(rwkv_k_a[i]), r_k=row(rwkv_r_k[i]),
        ln_w=row(rwkv_ln_w[i]), ln_b=row(rwkv_ln_b[i]),
        hsum=(lane[:, None] == lane[None, :]).astype(BF16),
    )
    pad_heads = lambda x: jnp.pad(x, (GDN_HEADS, GDN_BA_PAD - n_ba)).reshape(1, GDN_BA_PAD)
    gdn = dict(conv_w=gdn_conv_w[i], a_log=pad_heads(gdn_A_log[i]), dt_bias=pad_heads(gdn_dt_bias[i]),
               norm_w=row(gdn_norm_w[i]))
    wo = w_out_even[i].astype(BF16)
    return w_cat, rwkv, gdn, wo


def _prep_odd(i, w_in_odd, w_out_odd, lru_conv_w, lru_conv_b, lru_wa, lru_ba, lru_wi, lru_bi, lru_L):
    lru = dict(
        conv_w=lru_conv_w[i], conv_b=lru_conv_b[i].reshape(1, LRU_W),
        wab=jnp.concatenate([lru_wa[i], lru_wi[i]], axis=-1).astype(BF16),
        bab=jnp.concatenate([lru_ba[i], lru_bi[i]], axis=-1).reshape(LRU_HEADS, 1, 2 * LRU_BW),
        l=lru_L[i].reshape(1, LRU_W),
    )
    return w_in_odd[i].astype(BF16), lru, w_out_odd[i].astype(BF16)


def _row_tile(n):
    return 512 if n % 512 == 0 else 256


def _trunk(x, st_rwkv, st_shift, st_gdn, st_gconv, st_lru, st_lconv, even_p, odd_p, mlp_p, norm_mix,
           norm_mlp, norm_final):
    b, t, _ = x.shape
    n = b * t
    tm = _row_tile(n)
    c_even = min(64, t)
    c_odd = min(256, t)
    h = x.reshape(n, D_MODEL)
    n_rwkv, n_shift, n_gdn, n_gconv, n_lru, n_lconv = [], [], [], [], [], []
    y_final = None
    for l in range(DEPTH):
        i = l // 2
        up, down = mlp_p[l]
        final_nw = norm_final if l == DEPTH - 1 else None
        if l % 2 == 0:
            w_cat, rwkv, gdn, wo = even_p[i]
            p, qkv, z, ba = _norm_proj(h, norm_mix[l], w_cat, (RWKV_PROJ, GDN_QKV, GDN_W, GDN_BA_PAD), tm)
            p = p.reshape(b, t, RWKV_PROJ)
            qkv = qkv.reshape(b, t, GDN_QKV)
            ya, s_a = _rwkv_mix(p, st_shift[i], st_rwkv[i], rwkv, c_even)
            yb, s_b = _gdn_mix(qkv, z.reshape(b, t, GDN_W), ba.reshape(b, t, GDN_BA_PAD), st_gconv[i],
                               st_gdn[i], gdn, c_even)
            n_rwkv.append(s_a)
            n_shift.append(p[:, -1])
            n_gdn.append(s_b)
            n_gconv.append(qkv[:, t - (CONV_W - 1):])
            ys = (ya.reshape(n, RWKV_W), yb.reshape(n, GDN_W))
        else:
            w_in, lru, wo = odd_p[i]
            gate, xb = _norm_proj(h, norm_mix[l], w_in, (LRU_W, LRU_W), tm)
            xb = xb.reshape(b, t, LRU_W)
            yc, h_last = _lru_mix(gate.reshape(b, t, LRU_W), xb, st_lconv[i], st_lru[i], lru, c_odd)
            n_lru.append(h_last)
            n_lconv.append(xb[:, t - (CONV_W - 1):])
            ys = (yc.reshape(n, LRU_W),)
        outs = _mix_out_mlp(h, ys, wo, norm_mlp[l], up, down, final_nw, tm)
        h = outs[0]
        if final_nw is not None:
            y_final = outs[1]
    return (y_final.reshape(b, t, D_MODEL), jnp.stack(n_rwkv), jnp.stack(n_shift), jnp.stack(n_gdn),
            jnp.stack(n_gconv), jnp.stack(n_lru), jnp.stack(n_lconv))


def kernel(x_prompt, x_sample, state_rwkv, state_rwkv_shift, state_gdn, state_gdn_conv, state_lru, state_lru_conv, norm_mix, norm_mlp, norm_final, w_in_even, w_out_even, rwkv_mu, rwkv_w0, rwkv_w2, rwkv_a0, rwkv_a2, rwkv_g2, rwkv_k_k, rwkv_k_a, rwkv_r_k, rwkv_ln_w, rwkv_ln_b, gdn_conv_w, gdn_A_log, gdn_dt_bias, gdn_norm_w, w_in_odd, w_out_odd, lru_conv_w, lru_conv_b, lru_wa, lru_ba, lru_wi, lru_bi, lru_L, mlp_up, mlp_down):
    n_even = (DEPTH + 1) // 2
    n_odd = DEPTH // 2
    even_p = [_prep_even(i, w_in_even, w_out_even, rwkv_mu, rwkv_w0, rwkv_w2, rwkv_a0, rwkv_a2, rwkv_g2,
                         rwkv_k_k, rwkv_k_a, rwkv_r_k, rwkv_ln_w, rwkv_ln_b, gdn_conv_w, gdn_A_log,
                         gdn_dt_bias, gdn_norm_w) for i in range(n_even)]
    odd_p = [_prep_odd(i, w_in_odd, w_out_odd, lru_conv_w, lru_conv_b, lru_wa, lru_ba, lru_wi, lru_bi, lru_L)
             for i in range(n_odd)]
    mlp_p = [(mlp_up[l].astype(BF16), mlp_down[l].astype(BF16)) for l in range(DEPTH)]
    bp = x_prompt.shape[0]
    dt = x_prompt.dtype
    zeros = lambda *s: jnp.zeros(s, dt)
    shared = (even_p, odd_p, mlp_p, norm_mix, norm_mlp, norm_final)
    y_p, rwkv_p, shift_p, gdn_p, gconv_p, lru_p, lconv_p = _trunk(
        x_prompt,
        zeros(n_even, bp, RWKV_HEADS, RWKV_HD, RWKV_HD), zeros(n_even, bp, RWKV_PROJ),
        zeros(n_even, bp, GDN_HEADS, GDN_HD, GDN_HD), zeros(n_even, bp, CONV_W - 1, GDN_QKV),
        zeros(n_odd, bp, LRU_W), zeros(n_odd, bp, CONV_W - 1, LRU_W), *shared)
    y_s, rwkv_s, shift_s, gdn_s, gconv_s, lru_s, lconv_s = _trunk(
        x_sample, state_rwkv, state_rwkv_shift, state_gdn, state_gdn_conv, state_lru, state_lru_conv, *shared)
    return (y_p, y_s, rwkv_p, rwkv_s, shift_p, shift_s, gdn_p, gdn_s, gconv_p, gconv_s, lru_p, lru_s,
            lconv_p, lconv_s)
```

```python
import functools

import jax
import jax.numpy as jnp
from jax import lax
from jax.experimental import pallas as pl
from jax.experimental.pallas import tpu as pltpu

F32 = jnp.float32
BF16 = jnp.bfloat16

D_MODEL = 1024
DEPTH = 4
CONV_W = 4
NORM_EPS = 1e-6
RWKV_HEADS = 8
RWKV_HD = 64
RWKV_W = RWKV_HEADS * RWKV_HD
RWKV_PAIRS = RWKV_HEADS // 2
DECAY_LORA = 64
AAA_LORA = 64
GATE_LORA = 128
RWKV_PROJ = 3 * RWKV_W + DECAY_LORA + AAA_LORA + GATE_LORA
RWKV_GN_EPS = 64e-5
GDN_HEADS = 4
GDN_HD = 128
GDN_W = GDN_HEADS * GDN_HD
GDN_QKV = 3 * GDN_W
GDN_BA_PAD = 128
LRU_W = D_MODEL
LRU_HEADS = 8
LRU_BW = LRU_W // LRU_HEADS
LRU_C = 8.0
D_FF = 4 * D_MODEL

SUBLANES = 8
LANES = 128
VMEM_LIMIT_BYTES = 56 * 1024 * 1024


def _mm(a, b):
    return jnp.dot(a.astype(BF16), b.astype(BF16), preferred_element_type=F32)


def _mm_nt(a, b):
    return lax.dot_general(a.astype(BF16), b.astype(BF16), (((1,), (1,)), ((), ())),
                           preferred_element_type=F32)


def _mm_tn(a, b):
    return lax.dot_general(a.astype(BF16), b.astype(BF16), (((0,), (0,)), ((), ())),
                           preferred_element_type=F32)


def _split3(x):
    h1 = x.astype(BF16)
    r1 = x - h1.astype(F32)
    h2 = r1.astype(BF16)
    r2 = r1 - h2.astype(F32)
    return h1, h2, r2.astype(BF16)


def _mm_exact_lhs(m01, x):
    h1, h2, h3 = _split3(x)
    d = lambda t: jnp.dot(m01, t, preferred_element_type=F32)
    return d(h1) + d(h2) + d(h3)


def _mm_exact_rhs(x, m01):
    h1, h2, h3 = _split3(x)
    d = lambda t: jnp.dot(t, m01, preferred_element_type=F32)
    return d(h1) + d(h2) + d(h3)


def _mm3(a, b):
    ah = a.astype(BF16)
    al = (a - ah.astype(F32)).astype(BF16)
    bh = b.astype(BF16)
    bl = (b - bh.astype(F32)).astype(BF16)
    d = lambda x, y: jnp.dot(x, y, preferred_element_type=F32)
    return d(ah, bh) + d(ah, bl) + d(al, bh)


def _softplus(x):
    return jnp.maximum(x, 0.0) + jnp.log1p(jnp.exp(-jnp.abs(x)))


def _sigmoid(x):
    return jax.nn.sigmoid(x)


def _silu(x):
    return x * jax.nn.sigmoid(x)


def _gelu_tanh(x):
    return 0.5 * x * (1.0 + jnp.tanh(0.7978845608028654 * (x + 0.044715 * (x * x * x))))


def _block_masks(n, c):
    row = lax.broadcasted_iota(jnp.int32, (n, n), 0)
    col = lax.broadcasted_iota(jnp.int32, (n, n), 1)
    same = jnp.bitwise_xor(row, col) < c
    return same & (row >= col), same & (row > col), same, row == col


def _unit_lower_inverses(ms, eye, c):
    ps = [eye + m for m in ms]
    mks = list(ms)
    span = 2
    while span < c:
        mks = [_mm3(mk, mk) for mk in mks]
        ps = [p + _mm3(p, mk) for p, mk in zip(ps, mks)]
        span *= 2
    return ps


def _rows(x, s, c):
    return x[s * c:(s + 1) * c]


def _cat(xs, axis=0):
    return xs[0] if len(xs) == 1 else jnp.concatenate(xs, axis=axis)


def _const_spec(shape):
    nd = len(shape)
    return pl.BlockSpec(shape, lambda *_: (0,) * nd)


def _causal_conv(x_ref, xp_ref, cw_ref, s, c):
    base = SUBLANES
    nb = CONV_W - 1
    xp_ref[s, base:base + c, :] = x_ref[s]
    y = cw_ref[0:1, :] * xp_ref[s, base - nb:base - nb + c, :]
    for j in range(1, CONV_W):
        y = y + cw_ref[j:j + 1, :] * xp_ref[s, base - nb + j:base - nb + j + c, :]
    xp_ref[s, base - nb:base, :] = xp_ref[s, base + c - nb:base + c, :]
    return y


def _norm_proj_kernel(x_ref, nw_ref, w_ref, *o_refs, widths, col_chunk):
    x = x_ref[...]
    var = jnp.mean(x * x, axis=-1, keepdims=True)
    xn = (x * lax.rsqrt(var + NORM_EPS) * nw_ref[...]).astype(BF16)
    off = 0
    for o_ref, width in zip(o_refs, widths):
        for c0 in range(0, width, col_chunk):
            cw = min(col_chunk, width - c0)
            o_ref[:, c0:c0 + cw] = jnp.dot(xn, w_ref[:, off + c0:off + c0 + cw],
                                           preferred_element_type=F32)
        off += width


def _norm_proj(x, nw, w_bf16, widths, tm):
    n = x.shape[0]
    p = w_bf16.shape[1]
    assert sum(widths) == p and n % tm == 0
    kern = functools.partial(_norm_proj_kernel, widths=tuple(widths), col_chunk=512)
    return pl.pallas_call(
        kern,
        grid=(n // tm,),
        in_specs=[
            pl.BlockSpec((tm, D_MODEL), lambda i: (i, 0)),
            _const_spec((1, D_MODEL)),
            _const_spec((D_MODEL, p)),
        ],
        out_specs=[pl.BlockSpec((tm, wd), lambda i: (i, 0)) for wd in widths],
        out_shape=[jax.ShapeDtypeStruct((n, wd), F32) for wd in widths],
        compiler_params=pltpu.CompilerParams(
            dimension_semantics=("parallel",), vmem_limit_bytes=VMEM_LIMIT_BYTES),
        name="norm_proj",
    )(x, nw.reshape(1, D_MODEL), w_bf16)


def _mlp_kernel(*refs, n_mix, ff_chunk, final):
    h_ref = refs[0]
    y_refs = refs[1:1 + n_mix]
    wo_ref, nw_ref, up_ref, down_ref = refs[1 + n_mix:5 + n_mix]
    rest = refs[5 + n_mix:]
    if final:
        fnw_ref, o_ref, yf_ref = rest
    else:
        (o_ref,) = rest
    y = _cat([y_ref[...].astype(BF16) for y_ref in y_refs], axis=1)
    h = h_ref[...] + jnp.dot(y, wo_ref[...], preferred_element_type=F32)
    var = jnp.mean(h * h, axis=-1, keepdims=True)
    xn = (h * lax.rsqrt(var + NORM_EPS) * nw_ref[...]).astype(BF16)
    acc = h
    for c0 in range(0, D_FF, ff_chunk):
        u = jnp.dot(xn, up_ref[:, c0:c0 + ff_chunk], preferred_element_type=F32)
        r = jnp.square(jnp.maximum(u, 0.0)).astype(BF16)
        acc = acc + jnp.dot(r, down_ref[c0:c0 + ff_chunk, :], preferred_element_type=F32)
    o_ref[...] = acc
    if final:
        var2 = jnp.mean(acc * acc, axis=-1, keepdims=True)
        yf_ref[...] = acc * lax.rsqrt(var2 + NORM_EPS) * fnw_ref[...]


def _mix_out_mlp(h, ys, wo, nw, up, down, final_nw, tm):
    n = h.shape[0]
    n_mix = len(ys)
    final = final_nw is not None
    kern = functools.partial(_mlp_kernel, n_mix=n_mix, ff_chunk=512, final=final)
    row = lambda wd: pl.BlockSpec((tm, wd), lambda i: (i, 0))
    in_specs = [row(D_MODEL)] + [row(y.shape[1]) for y in ys]
    in_specs += [_const_spec(wo.shape), _const_spec((1, D_MODEL)), _const_spec((D_MODEL, D_FF)),
                 _const_spec((D_FF, D_MODEL))]
    args = [h, *ys, wo, nw.reshape(1, D_MODEL), up, down]
    out_specs = [row(D_MODEL)]
    out_shape = [jax.ShapeDtypeStruct((n, D_MODEL), F32)]
    if final:
        in_specs.append(_const_spec((1, D_MODEL)))
        args.append(final_nw.reshape(1, D_MODEL))
        out_specs.append(row(D_MODEL))
        out_shape.append(jax.ShapeDtypeStruct((n, D_MODEL), F32))
    return pl.pallas_call(
        kern,
        grid=(n // tm,),
        in_specs=in_specs,
        out_specs=out_specs,
        out_shape=out_shape,
        compiler_params=pltpu.CompilerParams(
            dimension_semantics=("parallel",), vmem_limit_bytes=VMEM_LIMIT_BYTES),
        name="mix_out_mlp",
    )(*args)


def _pair_rows(x, s, g, c):
    if g == 1:
        return x
    return jnp.concatenate([_rows(x, s, c), _rows(x, g + s, c)], axis=0)


def _pair_merge(pieces, c):
    return _cat([p[:c] for p in pieces] + [p[c:] for p in pieces])


def _rwkv_kernel(p_ref, prev_ref, s0_ref, mu_ref, w0_ref, w2_ref, a0_ref, a2_ref, g2_ref,
                 kk_ref, ka_ref, rk_ref, lnw_ref, lnb_ref, hsum_ref,
                 out_ref, s_out_ref, xp_ref, state_ref, *, c, g, nc):
    ci = pl.program_id(1)
    base = SUBLANES
    gc = g * c

    @pl.when(ci == 0)
    def _():
        for s in range(g):
            xp_ref[s, base - 1:base, :] = prev_ref[s]
        state_ref[...] = s0_ref[...]

    shifted = []
    for s in range(g):
        ps = p_ref[s]
        xp_ref[s, base:base + c, :] = ps
        shifted.append(xp_ref[s, base - 1:base - 1 + c, :])
        xp_ref[s, base - 1:base, :] = ps[c - 1:c, :]
    p = p_ref[...].reshape(gc, RWKV_PROJ)
    shifted = _cat(shifted)

    xs = p + mu_ref[...] * (shifted - p)
    r = xs[:, 0:RWKV_W]
    k = xs[:, RWKV_W:2 * RWKV_W]
    v = xs[:, 2 * RWKV_W:3 * RWKV_W]
    xwa = xs[:, 3 * RWKV_W:3 * RWKV_W + DECAY_LORA + AAA_LORA]
    xg = xs[:, 3 * RWKV_W + DECAY_LORA + AAA_LORA:]

    w_log = -_softplus(-(w0_ref[...] + _mm(jnp.tanh(xwa), w2_ref[...]))) - 0.5
    ld = -jnp.exp(w_log)
    a = _sigmoid(a0_ref[...] + _mm(xwa, a2_ref[...]))
    gate = _mm(_sigmoid(xg), g2_ref[...])

    incl1, _, same1, _ = _block_masks(gc, c)
    cum = _mm_exact_lhs(incl1.astype(BF16), ld)
    tot = _mm_exact_lhs(same1.astype(BF16), ld)
    w_in = jnp.exp(cum)
    w_ex = jnp.exp(cum - ld)
    w_inv = jnp.exp(-cum)
    w_tail = jnp.exp(tot - cum)
    w_last = jnp.exp(tot)

    hsum = hsum_ref[...]
    kkr = k * kk_ref[...]
    kk = kkr * lax.rsqrt(_mm_exact_rhs(kkr * kkr, hsum) + 1e-6)
    kp = k * (1.0 + (a - 1.0) * ka_ref[...])
    bv = kk * a

    lo = lax.broadcasted_iota(jnp.int32, (gc, LANES), 1) < RWKV_HD

    def pairs(x):
        out = []
        for j in range(RWKV_PAIRS):
            slab = x[:, j * LANES:(j + 1) * LANES]
            out.append(jnp.concatenate([jnp.where(lo, slab, 0.0), jnp.where(lo, 0.0, slab)], axis=0))
        return out

    a_t = pairs(-kk * w_ex)
    r_t = pairs(r * w_in)
    b_t = pairs(bv * w_inv)
    k_t = pairs(kp * w_inv)
    v_b = pairs(v)
    b_w = pairs(bv * w_tail)
    k_w = pairs(kp * w_tail)

    n2 = 2 * gc
    incl, strict, _, diag = _block_masks(n2, c)
    eye = diag.astype(F32)
    ars = [jnp.concatenate([a_t[j], r_t[j]], axis=0) for j in range(RWKV_PAIRS)]
    gbs = [_mm_nt(ars[j], b_t[j]) for j in range(RWKV_PAIRS)]
    gks = [_mm_nt(ars[j], k_t[j]) for j in range(RWKV_PAIRS)]
    m_ab = [jnp.where(strict, x[:n2], 0.0) for x in gbs]
    m_ak = [jnp.where(strict, x[:n2], 0.0) for x in gks]
    l_rb = [jnp.where(incl, x[n2:], 0.0) for x in gbs]
    l_rk = [jnp.where(incl, x[n2:], 0.0) for x in gks]
    t_inv = _unit_lower_inverses(m_ab, eye, c)

    sa, sr = [], []
    for j in range(RWKV_PAIRS):
        pieces = []
        for s in range(g):
            lhs = jnp.concatenate([_pair_rows(a_t[j], s, g, c), _pair_rows(r_t[j], s, g, c)], axis=0)
            pieces.append(_mm_nt(lhs, state_ref[s, j]))
        sa.append(_pair_merge([x[:2 * c] for x in pieces], c))
        sr.append(_pair_merge([x[2 * c:] for x in pieces], c))

    mv = [_mm(m_ak[j], v_b[j]) for j in range(RWKV_PAIRS)]
    us = [_mm3(t_inv[j], sa[j] + mv[j]) for j in range(RWKV_PAIRS)]
    ys = [sr[j] + _mm(l_rb[j], us[j]) + _mm(l_rk[j], v_b[j]) for j in range(RWKV_PAIRS)]
    y = jnp.concatenate([x[:gc] + x[gc:] for x in ys], axis=1)

    for j in range(RWKV_PAIRS):
        for s in range(g):
            uv = jnp.concatenate([_pair_rows(us[j], s, g, c), _pair_rows(v_b[j], s, g, c)], axis=0)
            bk = jnp.concatenate([_pair_rows(b_w[j], s, g, c), _pair_rows(k_w[j], s, g, c)], axis=0)
            decay = w_last[s * c:s * c + 1, j * LANES:(j + 1) * LANES]
            state_ref[s, j] = state_ref[s, j] * decay + _mm_tn(uv, bk)

    inv_hd = 1.0 / RWKV_HD
    mean = _mm_exact_rhs(y, hsum) * inv_hd
    d = y - mean
    var = _mm_exact_rhs(d * d, hsum) * inv_hd
    yn = d * lax.rsqrt(var + RWKV_GN_EPS) * lnw_ref[...] + lnb_ref[...]
    bonus = _mm_exact_rhs(r * kp * rk_ref[...], hsum)
    out_ref[...] = ((yn + bonus * v) * gate).reshape(g, c, RWKV_W)

    @pl.when(ci == nc - 1)
    def _():
        s_out_ref[...] = state_ref[...]


def _rwkv_mix(p, prev, s0_bd, prm, c, g):
    b, t, _ = p.shape
    nc = t // c
    assert nc * c == t and c % SUBLANES == 0 and b % g == 0
    kern = functools.partial(_rwkv_kernel, c=c, g=g, nc=nc)
    consts = [prm["mu"], prm["w0"], prm["w2"], prm["a0"], prm["a2"], prm["g2"], prm["k_k"], prm["k_a"],
              prm["r_k"], prm["ln_w"], prm["ln_b"], prm["hsum"]]
    st_block = (g, RWKV_PAIRS, LANES, LANES)
    return pl.pallas_call(
        kern,
        grid=(b // g, nc),
        in_specs=[
            pl.BlockSpec((g, c, RWKV_PROJ), lambda i, j: (i, j, 0)),
            pl.BlockSpec((g, 1, RWKV_PROJ), lambda i, j: (i, 0, 0)),
            pl.BlockSpec(st_block, lambda i, j: (i, 0, 0, 0)),
        ] + [_const_spec(x.shape) for x in consts],
        out_specs=[
            pl.BlockSpec((g, c, RWKV_W), lambda i, j: (i, j, 0)),
            pl.BlockSpec(st_block, lambda i, j: (i, 0, 0, 0)),
        ],
        out_shape=[
            jax.ShapeDtypeStruct((b, t, RWKV_W), F32),
            jax.ShapeDtypeStruct((b,) + st_block[1:], F32),
        ],
        scratch_shapes=[
            pltpu.VMEM((g, c + SUBLANES, RWKV_PROJ), F32),
            pltpu.VMEM(st_block, F32),
        ],
        compiler_params=pltpu.CompilerParams(
            dimension_semantics=("parallel", "arbitrary"), vmem_limit_bytes=VMEM_LIMIT_BYTES),
        name="rwkv7_mix",
    )(p, prev.reshape(b, 1, RWKV_PROJ), s0_bd, *consts)


def _heads_to_pairs(s):
    b = s.shape[0]
    s = s.reshape(b, RWKV_PAIRS, 2, RWKV_HD, RWKV_HD)
    z = jnp.zeros_like(s[:, :, 0])
    top = jnp.concatenate([s[:, :, 0], z], axis=-1)
    bot = jnp.concatenate([z, s[:, :, 1]], axis=-1)
    return jnp.concatenate([top, bot], axis=-2)


def _pairs_to_heads(sb):
    b = sb.shape[0]
    s = jnp.stack([sb[:, :, :RWKV_HD, :RWKV_HD], sb[:, :, RWKV_HD:, RWKV_HD:]], axis=2)
    return s.reshape(b, RWKV_HEADS, RWKV_HD, RWKV_HD)


def _gdn_kernel(qkv_ref, z_ref, ba_ref, buf_ref, s0_ref, cw_ref, alog_ref, dt_ref, nw_ref,
                out_ref, s_out_ref, xp_ref, state_ref, *, c, g, nc):
    ci = pl.program_id(1)
    base = SUBLANES
    nb = CONV_W - 1
    gc = g * c

    @pl.when(ci == 0)
    def _():
        for s in range(g):
            xp_ref[s, base - nb:base, :] = buf_ref[s]
        state_ref[...] = s0_ref[...]

    qkv = _silu(_cat([_causal_conv(qkv_ref, xp_ref, cw_ref, s, c) for s in range(g)]))
    z = z_ref[...].reshape(gc, GDN_W)
    ba = ba_ref[...].reshape(gc, GDN_BA_PAD)
    beta_all = _sigmoid(ba)
    g_all = -jnp.exp(alog_ref[...]) * _softplus(ba + dt_ref[...])
    incl, strict, same, diag = _block_masks(gc, c)
    eye = diag.astype(F32)
    gc_all = _mm_exact_lhs(incl.astype(BF16), g_all)
    tot_all = _mm_exact_lhs(same.astype(BF16), g_all)
    upper = (same & jnp.logical_not(strict)).astype(BF16)
    dn = (((0,), (0,)), ((), ()))
    gct = sum(lax.dot_general(t, upper, dn, preferred_element_type=F32) for t in _split3(g_all))

    hs = range(GDN_HEADS)
    head = lambda x, part, h: x[:, part * GDN_W + h * GDN_HD:part * GDN_W + (h + 1) * GDN_HD]
    q = [head(qkv, 0, h) for h in hs]
    k = [head(qkv, 1, h) for h in hs]
    v = [head(qkv, 2, h) for h in hs]
    q = [x * lax.rsqrt(jnp.sum(x * x, axis=-1, keepdims=True) + 1e-6) * (GDN_HD ** -0.5) for x in q]
    k = [x * lax.rsqrt(jnp.sum(x * x, axis=-1, keepdims=True) + 1e-6) for x in k]
    beta = [beta_all[:, h:h + 1] for h in hs]
    gcol = [gc_all[:, GDN_HEADS + h:GDN_HEADS + h + 1] for h in hs]
    grow = [gct[GDN_HEADS + h:GDN_HEADS + h + 1, :] for h in hs]
    glast = [tot_all[:, GDN_HEADS + h:GDN_HEADS + h + 1] for h in hs]
    decay = [jnp.where(incl, jnp.exp(jnp.where(incl, gcol[h] - grow[h], 0.0)), 0.0) for h in hs]
    kb = [k[h] * beta[h] for h in hs]
    vb = [v[h] * beta[h] for h in hs]
    gq = [_mm_nt(jnp.concatenate([q[h], kb[h]], axis=0), k[h]) for h in hs]
    qk = [jnp.where(incl, gq[h][:gc] * decay[h], 0.0) for h in hs]
    lmat = [jnp.where(strict, gq[h][gc:] * decay[h], 0.0) for h in hs]
    t_inv = _unit_lower_inverses([-x for x in lmat], eye, c)
    eg = [jnp.exp(gcol[h]) for h in hs]
    uw = [_mm3(t_inv[h], jnp.concatenate([vb[h], kb[h] * eg[h]], axis=1)) for h in hs]
    qe = [q[h] * eg[h] for h in hs]

    q_s, w_s = [], []
    for h in hs:
        pieces = []
        for s in range(g):
            lhs = jnp.concatenate([_rows(qe[h], s, c), _rows(uw[h][:, GDN_HD:], s, c)], axis=0)
            pieces.append(_mm(lhs, state_ref[s, h]))
        q_s.append(_cat([x[:c] for x in pieces]))
        w_s.append(_cat([x[c:] for x in pieces]))
    v_new = [uw[h][:, :GDN_HD] - w_s[h] for h in hs]
    o = [q_s[h] + _mm(qk[h], v_new[h]) for h in hs]
    for h in hs:
        k_tail = k[h] * jnp.exp(glast[h] - gcol[h])
        for s in range(g):
            decay_s = jnp.exp(glast[h][s * c:s * c + 1, :])
            state_ref[s, h] = state_ref[s, h] * decay_s + _mm_tn(_rows(k_tail, s, c), _rows(v_new[h], s, c))
    outs = []
    for h in hs:
        on = o[h] * lax.rsqrt(jnp.mean(o[h] * o[h], axis=-1, keepdims=True) + NORM_EPS) * nw_ref[...]
        outs.append(on * _silu(z[:, h * GDN_HD:(h + 1) * GDN_HD]))
    out_ref[...] = jnp.concatenate(outs, axis=1).reshape(g, c, GDN_W)

    @pl.when(ci == nc - 1)
    def _():
        s_out_ref[...] = state_ref[...]


def _gdn_mix(qkv, z, ba, buf, s0, prm, c, g):
    b, t, _ = qkv.shape
    nc = t // c
    assert nc * c == t and c % SUBLANES == 0 and b % g == 0
    kern = functools.partial(_gdn_kernel, c=c, g=g, nc=nc)
    consts = [prm["conv_w"], prm["a_log"], prm["dt_bias"], prm["norm_w"]]
    st_block = (g, GDN_HEADS, GDN_HD, GDN_HD)
    return pl.pallas_call(
        kern,
        grid=(b // g, nc),
        in_specs=[
            pl.BlockSpec((g, c, GDN_QKV), lambda i, j: (i, j, 0)),
            pl.BlockSpec((g, c, GDN_W), lambda i, j: (i, j, 0)),
            pl.BlockSpec((g, c, GDN_BA_PAD), lambda i, j: (i, j, 0)),
            pl.BlockSpec((g, CONV_W - 1, GDN_QKV), lambda i, j: (i, 0, 0)),
            pl.BlockSpec(st_block, lambda i, j: (i, 0, 0, 0)),
        ] + [_const_spec(x.shape) for x in consts],
        out_specs=[
            pl.BlockSpec((g, c, GDN_W), lambda i, j: (i, j, 0)),
            pl.BlockSpec(st_block, lambda i, j: (i, 0, 0, 0)),
        ],
        out_shape=[
            jax.ShapeDtypeStruct((b, t, GDN_W), F32),
            jax.ShapeDtypeStruct((b,) + st_block[1:], F32),
        ],
        scratch_shapes=[
            pltpu.VMEM((g, c + SUBLANES, GDN_QKV), F32),
            pltpu.VMEM(st_block, F32),
        ],
        compiler_params=pltpu.CompilerParams(
            dimension_semantics=("parallel", "arbitrary"), vmem_limit_bytes=VMEM_LIMIT_BYTES),
        name="gdn_mix",
    )(qkv, z, ba, buf, s0, *consts)


def _lru_kernel(gate_ref, xb_ref, buf_ref, h0_ref, cw_ref, cb_ref, wab_ref, bab_ref, l_ref,
                out_ref, h_out_ref, xp_ref, carry_ref, sa_ref, sb_ref, *, c, g, nc, pad):
    ci = pl.program_id(1)
    base = SUBLANES
    nb = CONV_W - 1
    gc = g * c

    @pl.when(ci == 0)
    def _():
        for s in range(g):
            xp_ref[s, base - nb:base, :] = buf_ref[s]
        carry_ref[...] = h0_ref[...]
        sa_ref[0:pad, :] = jnp.ones((pad, LRU_W), F32)
        sb_ref[0:pad, :] = jnp.zeros((pad, LRU_W), F32)

    xc = _cat([_causal_conv(xb_ref, xp_ref, cw_ref, s, c) for s in range(g)]) + cb_ref[...]

    for h in range(LRU_HEADS):
        sl = slice(h * LRU_BW, (h + 1) * LRU_BW)
        xh = xc[:, sl]
        ri = _mm(xh, wab_ref[h]) + bab_ref[h]
        r = _sigmoid(ri[:, :LRU_BW])
        i = _sigmoid(ri[:, LRU_BW:])
        log_a = -LRU_C * r * _softplus(-l_ref[:, sl])
        a = jnp.exp(log_a)
        mult = jnp.sqrt(-jnp.tanh(log_a) * (a * a + 1.0))
        sa_ref[pad:pad + gc, sl] = a
        sb_ref[pad:pad + gc, sl] = mult * i * xh

    tpos = jnp.bitwise_and(lax.broadcasted_iota(jnp.int32, (gc, 1), 0), c - 1)
    a = sa_ref[pad:pad + gc, :]
    b = sb_ref[pad:pad + gc, :]
    d = 1
    while d < c:
        live = tpos >= d
        a_sh = jnp.where(live, sa_ref[pad - d:pad - d + gc, :], 1.0)
        b_sh = jnp.where(live, sb_ref[pad - d:pad - d + gc, :], 0.0)
        b = a * b_sh + b
        a = a * a_sh
        d *= 2
        if d < c:
            sa_ref[pad:pad + gc, :] = a
            sb_ref[pad:pad + gc, :] = b
    carry = _cat([jnp.broadcast_to(carry_ref[s], (c, LRU_W)) for s in range(g)])
    hs = a * carry + b
    for s in range(g):
        carry_ref[s] = hs[s * c + c - 1:s * c + c, :]
    out_ref[...] = (hs * _gelu_tanh(gate_ref[...].reshape(gc, LRU_W))).reshape(g, c, LRU_W)

    @pl.when(ci == nc - 1)
    def _():
        h_out_ref[...] = carry_ref[...]


def _lru_mix(gate, xb, buf, h0, prm, c, g):
    b, t, _ = xb.shape
    nc = t // c
    assert nc * c == t and c % SUBLANES == 0 and b % g == 0
    pad = max(c // 2, SUBLANES)
    kern = functools.partial(_lru_kernel, c=c, g=g, nc=nc, pad=pad)
    consts = [prm["conv_w"], prm["conv_b"], prm["wab"], prm["bab"], prm["l"]]
    seq = lambda rows: pl.BlockSpec((g, rows, LRU_W), lambda i, j: (i, 0, 0))
    chunk = pl.BlockSpec((g, c, LRU_W), lambda i, j: (i, j, 0))
    y, h_last = pl.pallas_call(
        kern,
        grid=(b // g, nc),
        in_specs=[chunk, chunk, seq(CONV_W - 1), seq(1)] + [_const_spec(x.shape) for x in consts],
        out_specs=[chunk, seq(1)],
        out_shape=[
            jax.ShapeDtypeStruct((b, t, LRU_W), F32),
            jax.ShapeDtypeStruct((b, 1, LRU_W), F32),
        ],
        scratch_shapes=[
            pltpu.VMEM((g, c + SUBLANES, LRU_W), F32),
            pltpu.VMEM((g, 1, LRU_W), F32),
            pltpu.VMEM((pad + g * c, LRU_W), F32),
            pltpu.VMEM((pad + g * c, LRU_W), F32),
        ],
        compiler_params=pltpu.CompilerParams(
            dimension_semantics=("parallel", "arbitrary"), vmem_limit_bytes=VMEM_LIMIT_BYTES),
        name="lru_mix",
    )(gate, xb, buf, h0.reshape(b, 1, LRU_W), *consts)
    return y, h_last.reshape(b, LRU_W)


def _prep_even(i, w_in_even, w_out_even, rwkv_mu, rwkv_w0, rwkv_w2, rwkv_a0, rwkv_a2, rwkv_g2, rwkv_k_k,
               rwkv_k_a, rwkv_r_k, rwkv_ln_w, rwkv_ln_b, gdn_conv_w, gdn_A_log, gdn_dt_bias, gdn_norm_w):
    w = w_in_even[i]
    o = RWKV_PROJ
    n_ba = 2 * GDN_HEADS
    w_ba = jnp.pad(w[:, o + GDN_QKV + GDN_W:], ((0, 0), (0, GDN_BA_PAD - n_ba)))
    w_cat = jnp.concatenate([w[:, :o + GDN_QKV + GDN_W], w_ba], axis=1).astype(BF16)
    row = lambda x: x.reshape(1, -1)
    lane = jnp.arange(RWKV_W) // RWKV_HD
    lora = DECAY_LORA + AAA_LORA
    rwkv = dict(
        mu=row(rwkv_mu[i]), w0=row(rwkv_w0[i]), a0=row(rwkv_a0[i]),
        w2=jnp.pad(rwkv_w2[i], ((0, lora - DECAY_LORA), (0, 0))).astype(BF16),
        a2=jnp.pad(rwkv_a2[i], ((DECAY_LORA, 0), (0, 0))).astype(BF16),
        g2=rwkv_g2[i].astype(BF16),
        k_k=row(rwkv_k_k[i]), k_a=row(rwkv_k_a[i]), r_k=row(rwkv_r_k[i]),
        ln_w=row(rwkv_ln_w[i]), ln_b=row(rwkv_ln_b[i]),
        hsum=(lane[:, None] == lane[None, :]).astype(BF16),
    )
    pad_heads = lambda x: jnp.pad(x, (GDN_HEADS, GDN_BA_PAD - n_ba)).reshape(1, GDN_BA_PAD)
    gdn = dict(conv_w=gdn_conv_w[i], a_log=pad_heads(gdn_A_log[i]), dt_bias=pad_heads(gdn_dt_bias[i]),
               norm_w=row(gdn_norm_w[i]))
    return w_cat, rwkv, gdn, w_out_even[i].astype(BF16)


def _prep_odd(i, w_in_odd, w_out_odd, lru_conv_w, lru_conv_b, lru_wa, lru_ba, lru_wi, lru_bi, lru_L):
    lru = dict(
        conv_w=lru_conv_w[i], conv_b=lru_conv_b[i].reshape(1, LRU_W),
        wab=jnp.concatenate([lru_wa[i], lru_wi[i]], axis=-1).astype(BF16),
        bab=jnp.concatenate([lru_ba[i], lru_bi[i]], axis=-1).reshape(LRU_HEADS, 1, 2 * LRU_BW),
        l=lru_L[i].reshape(1, LRU_W),
    )
    return w_in_odd[i].astype(BF16), lru, w_out_odd[i].astype(BF16)


def _row_tile(n):
    return 512 if n % 512 == 0 else 256


def _seq_group(b, c, rows):
    g = max(1, min(b, rows // c))
    while b % g:
        g -= 1
    return g


def _trunk(x, st_rwkv, st_shift, st_gdn, st_gconv, st_lru, st_lconv, even_p, odd_p, mlp_p, norm_mix,
           norm_mlp, norm_final):
    b, t, _ = x.shape
    n = b * t
    tm = _row_tile(n)
    c_even = min(64, t)
    g_even = _seq_group(b, c_even, 64)
    c_odd = min(256, t)
    g_odd = _seq_group(b, c_odd, 256)
    h = x.reshape(n, D_MODEL)
    n_rwkv, n_shift, n_gdn, n_gconv, n_lru, n_lconv = [], [], [], [], [], []
    y_final = None
    for l in range(DEPTH):
        i = l // 2
        up, down = mlp_p[l]
        final_nw = norm_final if l == DEPTH - 1 else None
        if l % 2 == 0:
            w_cat, rwkv, gdn, wo = even_p[i]
            p, qkv, z, ba = _norm_proj(h, norm_mix[l], w_cat, (RWKV_PROJ, GDN_QKV, GDN_W, GDN_BA_PAD), tm)
            p = p.reshape(b, t, RWKV_PROJ)
            qkv = qkv.reshape(b, t, GDN_QKV)
            ya, s_a = _rwkv_mix(p, st_shift[i], _heads_to_pairs(st_rwkv[i]), rwkv, c_even, g_even)
            yb, s_b = _gdn_mix(qkv, z.reshape(b, t, GDN_W), ba.reshape(b, t, GDN_BA_PAD), st_gconv[i],
                               st_gdn[i], gdn, c_even, g_even)
            n_rwkv.append(_pairs_to_heads(s_a))
            n_shift.append(p[:, -1])
            n_gdn.append(s_b)
            n_gconv.append(qkv[:, t - (CONV_W - 1):])
            ys = (ya.reshape(n, RWKV_W), yb.reshape(n, GDN_W))
        else:
            w_in, lru, wo = odd_p[i]
            gate, xb = _norm_proj(h, norm_mix[l], w_in, (LRU_W, LRU_W), tm)
            xb = xb.reshape(b, t, LRU_W)
            yc, h_last = _lru_mix(gate.reshape(b, t, LRU_W), xb, st_lconv[i], st_lru[i], lru, c_odd, g_odd)
            n_lru.append(h_last)
            n_lconv.append(xb[:, t - (CONV_W - 1):])
            ys = (yc.reshape(n, LRU_W),)
        outs = _mix_out_mlp(h, ys, wo, norm_mlp[l], up, down, final_nw, tm)
        h = outs[0]
        if final_nw is not None:
            y_final = outs[1]
    return (y_final.reshape(b, t, D_MODEL), jnp.stack(n_rwkv), jnp.stack(n_shift), jnp.stack(n_gdn),
            jnp.stack(n_gconv), jnp.stack(n_lru), jnp.stack(n_lconv))


def kernel(x_prompt, x_sample, state_rwkv, state_rwkv_shift, state_gdn, state_gdn_conv, state_lru, state_lru_conv, norm_mix, norm_mlp, norm_final, w_in_even, w_out_even, rwkv_mu, rwkv_w0, rwkv_w2, rwkv_a0, rwkv_a2, rwkv_g2, rwkv_k_k, rwkv_k_a, rwkv_r_k, rwkv_ln_w, rwkv_ln_b, gdn_conv_w, gdn_A_log, gdn_dt_bias, gdn_norm_w, w_in_odd, w_out_odd, lru_conv_w, lru_conv_b, lru_wa, lru_ba, lru_wi, lru_bi, lru_L, mlp_up, mlp_down):
    n_even = (DEPTH + 1) // 2
    n_odd = DEPTH // 2
    even_p = [_prep_even(i, w_in_even, w_out_even, rwkv_mu, rwkv_w0, rwkv_w2, rwkv_a0, rwkv_a2, rwkv_g2,
                         rwkv_k_k, rwkv_k_a, rwkv_r_k, rwkv_ln_w, rwkv_ln_b, gdn_conv_w, gdn_A_log,
                         gdn_dt_bias, gdn_norm_w) for i in range(n_even)]
    odd_p = [_prep_odd(i, w_in_odd, w_out_odd, lru_conv_w, lru_conv_b, lru_wa, lru_ba, lru_wi, lru_bi, lru_L)
             for i in range(n_odd)]
    mlp_p = [(mlp_up[l].astype(BF16), mlp_down[l].astype(BF16)) for l in range(DEPTH)]
    bp = x_prompt.shape[0]
    dt = x_prompt.dtype
    zeros = lambda *s: jnp.zeros(s, dt)
    shared = (even_p, odd_p, mlp_p, norm_mix, norm_mlp, norm_final)
    y_p, rwkv_p, shift_p, gdn_p, gconv_p, lru_p, lconv_p = _trunk(
        x_prompt,
        zeros(n_even, bp, RWKV_HEADS, RWKV_HD, RWKV_HD), zeros(n_even, bp, RWKV_PROJ),
        zeros(n_even, bp, GDN_HEADS, GDN_HD, GDN_HD), zeros(n_even, bp, CONV_W - 1, GDN_QKV),
        zeros(n_odd, bp, LRU_W), zeros(n_odd, bp, CONV_W - 1, LRU_W), *shared)
    y_s, rwkv_s, shift_s, gdn_s, gconv_s, lru_s, lconv_s = _trunk(
        x_sample, state_rwkv, state_rwkv_shift, state_gdn, state_gdn_conv, state_lru, state_lru_conv, *shared)
    return (y_p, y_s, rwkv_p, rwkv_s, shift_p, shift_s, gdn_p, gdn_s, gconv_p, gconv_s, lru_p, lru_s,
            lconv_p, lconv_s)
```

```python
import functools

import jax
import jax.numpy as jnp
from jax import lax
from jax.experimental import pallas as pl
from jax.experimental.pallas import tpu as pltpu

F32 = jnp.float32
BF16 = jnp.bfloat16

D_MODEL = 1024
DEPTH = 4
CONV_W = 4
NORM_EPS = 1e-6
RWKV_HEADS = 8
RWKV_HD = 64
RWKV_W = RWKV_HEADS * RWKV_HD
RWKV_PAIRS = RWKV_HEADS // 2
DECAY_LORA = 64
AAA_LORA = 64
GATE_LORA = 128
RWKV_PROJ = 3 * RWKV_W + DECAY_LORA + AAA_LORA + GATE_LORA
RWKV_GN_EPS = 64e-5
GDN_HEADS = 4
GDN_HD = 128
GDN_W = GDN_HEADS * GDN_HD
GDN_QKV = 3 * GDN_W
GDN_BA_PAD = 128
LRU_W = D_MODEL
LRU_HEADS = 8
LRU_BW = LRU_W // LRU_HEADS
LRU_C = 8.0
D_FF = 4 * D_MODEL

SUBLANES = 8
LANES = 128
VMEM_LIMIT_BYTES = 56 * 1024 * 1024


def _mm(a, b):
    return jnp.dot(a.astype(BF16), b.astype(BF16), preferred_element_type=F32)


def _mm_nt(a, b):
    return lax.dot_general(a.astype(BF16), b.astype(BF16), (((1,), (1,)), ((), ())),
                           preferred_element_type=F32)


def _mm_tn(a, b):
    return lax.dot_general(a.astype(BF16), b.astype(BF16), (((0,), (0,)), ((), ())),
                           preferred_element_type=F32)


def _split3(x):
    h1 = x.astype(BF16)
    r1 = x - h1.astype(F32)
    h2 = r1.astype(BF16)
    r2 = r1 - h2.astype(F32)
    return h1, h2, r2.astype(BF16)


def _mm_exact_lhs(m01, x):
    h1, h2, h3 = _split3(x)
    d = lambda t: jnp.dot(m01, t, preferred_element_type=F32)
    return d(h1) + d(h2) + d(h3)


def _mm_exact_rhs(x, m01):
    h1, h2, h3 = _split3(x)
    d = lambda t: jnp.dot(t, m01, preferred_element_type=F32)
    return d(h1) + d(h2) + d(h3)


def _mm3(a, b):
    ah = a.astype(BF16)
    al = (a - ah.astype(F32)).astype(BF16)
    bh = b.astype(BF16)
    bl = (b - bh.astype(F32)).astype(BF16)
    d = lambda x, y: jnp.dot(x, y, preferred_element_type=F32)
    return d(ah, bh) + d(ah, bl) + d(al, bh)


def _softplus(x):
    return jnp.maximum(x, 0.0) + jnp.log1p(jnp.exp(-jnp.abs(x)))


def _sigmoid(x):
    return jax.nn.sigmoid(x)


def _silu(x):
    return x * jax.nn.sigmoid(x)


def _gelu_tanh(x):
    return 0.5 * x * (1.0 + jnp.tanh(0.7978845608028654 * (x + 0.044715 * (x * x * x))))


def _block_masks(n, c):
    row = lax.broadcasted_iota(jnp.int32, (n, n), 0)
    col = lax.broadcasted_iota(jnp.int32, (n, n), 1)
    same = jnp.bitwise_xor(row, col) < c
    return same & (row >= col), same & (row > col), same, row == col


def _unit_lower_inverses(ms, eye, c):
    n = eye.shape[0]
    row = lax.broadcasted_iota(jnp.int32, (n, n), 0)
    col = lax.broadcasted_iota(jnp.int32, (n, n), 1)
    corner = lambda s: ((jnp.bitwise_xor(row, col) < 2 * s) & (jnp.bitwise_and(row, s) != 0)
                        & (jnp.bitwise_and(col, s) == 0))
    ps = [eye + jnp.where(corner(1), m, 0.0) for m in ms]
    s = 2
    while s < c:
        mask = corner(s)
        pb = [_mm(p, jnp.where(mask, m, 0.0)) for p, m in zip(ps, ms)]
        ps = [p + _mm(x, p) for p, x in zip(ps, pb)]
        s *= 2
    return ps


def _rows(x, s, c):
    return x[s * c:(s + 1) * c]


def _cat(xs, axis=0):
    return xs[0] if len(xs) == 1 else jnp.concatenate(xs, axis=axis)


def _const_spec(shape):
    nd = len(shape)
    return pl.BlockSpec(shape, lambda *_: (0,) * nd)


def _causal_conv(x_ref, xp_ref, cw_ref, s, c):
    base = SUBLANES
    nb = CONV_W - 1
    xp_ref[s, base:base + c, :] = x_ref[s]
    y = cw_ref[0:1, :] * xp_ref[s, base - nb:base - nb + c, :]
    for j in range(1, CONV_W):
        y = y + cw_ref[j:j + 1, :] * xp_ref[s, base - nb + j:base - nb + j + c, :]
    xp_ref[s, base - nb:base, :] = xp_ref[s, base + c - nb:base + c, :]
    return y


def _norm_proj_kernel(x_ref, nw_ref, w_ref, *o_refs, widths, col_chunk):
    x = x_ref[...]
    var = jnp.mean(x * x, axis=-1, keepdims=True)
    xn = (x * lax.rsqrt(var + NORM_EPS) * nw_ref[...]).astype(BF16)
    off = 0
    for o_ref, width in zip(o_refs, widths):
        for c0 in range(0, width, col_chunk):
            cw = min(col_chunk, width - c0)
            o_ref[:, c0:c0 + cw] = jnp.dot(xn, w_ref[:, off + c0:off + c0 + cw],
                                           preferred_element_type=F32)
        off += width


def _norm_proj(x, nw, w_bf16, widths, tm):
    n = x.shape[0]
    p = w_bf16.shape[1]
    assert sum(widths) == p and n % tm == 0
    kern = functools.partial(_norm_proj_kernel, widths=tuple(widths), col_chunk=512)
    return pl.pallas_call(
        kern,
        grid=(n // tm,),
        in_specs=[
            pl.BlockSpec((tm, D_MODEL), lambda i: (i, 0)),
            _const_spec((1, D_MODEL)),
            _const_spec((D_MODEL, p)),
        ],
        out_specs=[pl.BlockSpec((tm, wd), lambda i: (i, 0)) for wd in widths],
        out_shape=[jax.ShapeDtypeStruct((n, wd), F32) for wd in widths],
        compiler_params=pltpu.CompilerParams(
            dimension_semantics=("parallel",), vmem_limit_bytes=VMEM_LIMIT_BYTES),
        name="norm_proj",
    )(x, nw.reshape(1, D_MODEL), w_bf16)


def _mlp_kernel(*refs, n_mix, ff_chunk, final):
    h_ref = refs[0]
    y_refs = refs[1:1 + n_mix]
    wo_ref, nw_ref, up_ref, down_ref = refs[1 + n_mix:5 + n_mix]
    rest = refs[5 + n_mix:]
    if final:
        fnw_ref, o_ref, yf_ref = rest
    else:
        (o_ref,) = rest
    y = _cat([y_ref[...].astype(BF16) for y_ref in y_refs], axis=1)
    h = h_ref[...] + jnp.dot(y, wo_ref[...], preferred_element_type=F32)
    var = jnp.mean(h * h, axis=-1, keepdims=True)
    xn = (h * lax.rsqrt(var + NORM_EPS) * nw_ref[...]).astype(BF16)
    acc = h
    for c0 in range(0, D_FF, ff_chunk):
        u = jnp.dot(xn, up_ref[:, c0:c0 + ff_chunk], preferred_element_type=F32)
        r = jnp.square(jnp.maximum(u, 0.0)).astype(BF16)
        acc = acc + jnp.dot(r, down_ref[c0:c0 + ff_chunk, :], preferred_element_type=F32)
    o_ref[...] = acc
    if final:
        var2 = jnp.mean(acc * acc, axis=-1, keepdims=True)
        yf_ref[...] = acc * lax.rsqrt(var2 + NORM_EPS) * fnw_ref[...]


def _mix_out_mlp(h, ys, wo, nw, up, down, final_nw, tm):
    n = h.shape[0]
    n_mix = len(ys)
    final = final_nw is not None
    kern = functools.partial(_mlp_kernel, n_mix=n_mix, ff_chunk=512, final=final)
    row = lambda wd: pl.BlockSpec((tm, wd), lambda i: (i, 0))
    in_specs = [row(D_MODEL)] + [row(y.shape[1]) for y in ys]
    in_specs += [_const_spec(wo.shape), _const_spec((1, D_MODEL)), _const_spec((D_MODEL, D_FF)),
                 _const_spec((D_FF, D_MODEL))]
    args = [h, *ys, wo, nw.reshape(1, D_MODEL), up, down]
    out_specs = [row(D_MODEL)]
    out_shape = [jax.ShapeDtypeStruct((n, D_MODEL), F32)]
    if final:
        in_specs.append(_const_spec((1, D_MODEL)))
        args.append(final_nw.reshape(1, D_MODEL))
        out_specs.append(row(D_MODEL))
        out_shape.append(jax.ShapeDtypeStruct((n, D_MODEL), F32))
    return pl.pallas_call(
        kern,
        grid=(n // tm,),
        in_specs=in_specs,
        out_specs=out_specs,
        out_shape=out_shape,
        compiler_params=pltpu.CompilerParams(
            dimension_semantics=("parallel",), vmem_limit_bytes=VMEM_LIMIT_BYTES),
        name="mix_out_mlp",
    )(*args)


def _pair_rows(x, s, g, c):
    if g == 1:
        return x
    return jnp.concatenate([_rows(x, s, c), _rows(x, g + s, c)], axis=0)


def _pair_merge(pieces, c):
    return _cat([p[:c] for p in pieces] + [p[c:] for p in pieces])


def _rwkv_kernel(p_ref, prev_ref, s0_ref, mu_ref, w0_ref, w2_ref, a0_ref, a2_ref, g2_ref,
                 kk_ref, ka_ref, rk_ref, lnw_ref, lnb_ref, hsum_ref,
                 out_ref, s_out_ref, xp_ref, state_ref, *, c, g, nc):
    ci = pl.program_id(1)
    base = SUBLANES
    gc = g * c

    hd = RWKV_HD

    @pl.when(ci == 0)
    def _():
        for s in range(g):
            xp_ref[s, base - 1:base, :] = prev_ref[s]
        state_ref[...] = jnp.zeros(state_ref.shape, F32)
        for s in range(g):
            for j in range(RWKV_PAIRS):
                state_ref[s, j, 0:hd, 0:hd] = s0_ref[s, 2 * j]
                state_ref[s, j, hd:2 * hd, hd:2 * hd] = s0_ref[s, 2 * j + 1]

    shifted = []
    for s in range(g):
        ps = p_ref[s]
        xp_ref[s, base:base + c, :] = ps
        shifted.append(xp_ref[s, base - 1:base - 1 + c, :])
        xp_ref[s, base - 1:base, :] = ps[c - 1:c, :]
    p = p_ref[...].reshape(gc, RWKV_PROJ)
    shifted = _cat(shifted)

    xs = p + mu_ref[...] * (shifted - p)
    r = xs[:, 0:RWKV_W]
    k = xs[:, RWKV_W:2 * RWKV_W]
    v = xs[:, 2 * RWKV_W:3 * RWKV_W]
    xwa = xs[:, 3 * RWKV_W:3 * RWKV_W + DECAY_LORA + AAA_LORA]
    xg = xs[:, 3 * RWKV_W + DECAY_LORA + AAA_LORA:]

    w_log = -_softplus(-(w0_ref[...] + _mm(jnp.tanh(xwa), w2_ref[...]))) - 0.5
    ld = -jnp.exp(w_log)
    a = _sigmoid(a0_ref[...] + _mm(xwa, a2_ref[...]))
    gate = _mm(_sigmoid(xg), g2_ref[...])

    incl1, _, same1, _ = _block_masks(gc, c)
    sums = _mm_exact_lhs(jnp.concatenate([incl1, same1], axis=0).astype(BF16), ld)
    cum = sums[:gc]
    tot = sums[gc:]
    w_in = jnp.exp(cum)
    w_ex = jnp.exp(cum - ld)
    w_inv = jnp.exp(-cum)
    w_tail = jnp.exp(tot - cum)
    w_last = jnp.exp(tot)

    hsum = hsum_ref[...]

    def head_sums(x):
        return jnp.concatenate(
            [_mm_exact_rhs(x[:, j * LANES:(j + 1) * LANES], hsum) for j in range(RWKV_PAIRS)], axis=1)

    kkr = k * kk_ref[...]
    kk = kkr * lax.rsqrt(head_sums(kkr * kkr) + 1e-6)
    kp = k * (1.0 + (a - 1.0) * ka_ref[...])
    bv = kk * a

    lo = lax.broadcasted_iota(jnp.int32, (gc, LANES), 1) < RWKV_HD

    def pairs(x):
        out = []
        for j in range(RWKV_PAIRS):
            slab = x[:, j * LANES:(j + 1) * LANES]
            out.append(jnp.concatenate([jnp.where(lo, slab, 0.0), jnp.where(lo, 0.0, slab)], axis=0))
        return out

    a_t = pairs(-kk * w_ex)
    r_t = pairs(r * w_in)
    b_t = pairs(bv * w_inv)
    k_t = pairs(kp * w_inv)
    v_b = pairs(v)
    b_w = pairs(bv * w_tail)
    k_w = pairs(kp * w_tail)

    n2 = 2 * gc
    incl, strict, _, diag = _block_masks(n2, c)
    eye = diag.astype(F32)
    ars = [jnp.concatenate([a_t[j], r_t[j]], axis=0) for j in range(RWKV_PAIRS)]
    gbs = [_mm_nt(ars[j], b_t[j]) for j in range(RWKV_PAIRS)]
    gks = [_mm_nt(ars[j], k_t[j]) for j in range(RWKV_PAIRS)]
    m_ab = [jnp.where(strict, x[:n2], 0.0) for x in gbs]
    m_ak = [jnp.where(strict, x[:n2], 0.0) for x in gks]
    l_rb = [jnp.where(incl, x[n2:], 0.0) for x in gbs]
    l_rk = [jnp.where(incl, x[n2:], 0.0) for x in gks]
    t_inv = _unit_lower_inverses(m_ab, eye, c)

    sa, sr = [], []
    for j in range(RWKV_PAIRS):
        pieces = []
        for s in range(g):
            lhs = jnp.concatenate([_pair_rows(a_t[j], s, g, c), _pair_rows(r_t[j], s, g, c)], axis=0)
            pieces.append(_mm_nt(lhs, state_ref[s, j]))
        sa.append(_pair_merge([x[:2 * c] for x in pieces], c))
        sr.append(_pair_merge([x[2 * c:] for x in pieces], c))

    mv = [_mm(m_ak[j], v_b[j]) for j in range(RWKV_PAIRS)]
    us = [_mm3(t_inv[j], sa[j] + mv[j]) for j in range(RWKV_PAIRS)]
    ys = [sr[j] + _mm(l_rb[j], us[j]) + _mm(l_rk[j], v_b[j]) for j in range(RWKV_PAIRS)]
    y = jnp.concatenate([x[:gc] + x[gc:] for x in ys], axis=1)

    for j in range(RWKV_PAIRS):
        for s in range(g):
            uv = jnp.concatenate([_pair_rows(us[j], s, g, c), _pair_rows(v_b[j], s, g, c)], axis=0)
            bk = jnp.concatenate([_pair_rows(b_w[j], s, g, c), _pair_rows(k_w[j], s, g, c)], axis=0)
            decay = w_last[s * c:s * c + 1, j * LANES:(j + 1) * LANES]
            state_ref[s, j] = state_ref[s, j] * decay + _mm_tn(uv, bk)

    inv_hd = 1.0 / RWKV_HD
    sums = head_sums(jnp.concatenate([y, y * y, r * kp * rk_ref[...]], axis=0))
    mean = sums[:gc] * inv_hd
    var = sums[gc:2 * gc] * inv_hd - mean * mean
    yn = (y - mean) * lax.rsqrt(var + RWKV_GN_EPS) * lnw_ref[...] + lnb_ref[...]
    out_ref[...] = ((yn + sums[2 * gc:] * v) * gate).reshape(g, c, RWKV_W)

    @pl.when(ci == nc - 1)
    def _():
        for s in range(g):
            for j in range(RWKV_PAIRS):
                s_out_ref[s, 2 * j] = state_ref[s, j, 0:hd, 0:hd]
                s_out_ref[s, 2 * j + 1] = state_ref[s, j, hd:2 * hd, hd:2 * hd]


def _rwkv_mix(p, prev, s0, prm, c, g):
    b, t, _ = p.shape
    nc = t // c
    assert nc * c == t and c % SUBLANES == 0 and b % g == 0
    kern = functools.partial(_rwkv_kernel, c=c, g=g, nc=nc)
    consts = [prm["mu"], prm["w0"], prm["w2"], prm["a0"], prm["a2"], prm["g2"], prm["k_k"], prm["k_a"],
              prm["r_k"], prm["ln_w"], prm["ln_b"], prm["hsum"]]
    st_block = (g, RWKV_HEADS, RWKV_HD, RWKV_HD)
    return pl.pallas_call(
        kern,
        grid=(b // g, nc),
        in_specs=[
            pl.BlockSpec((g, c, RWKV_PROJ), lambda i, j: (i, j, 0)),
            pl.BlockSpec((g, 1, RWKV_PROJ), lambda i, j: (i, 0, 0)),
            pl.BlockSpec(st_block, lambda i, j: (i, 0, 0, 0)),
        ] + [_const_spec(x.shape) for x in consts],
        out_specs=[
            pl.BlockSpec((g, c, RWKV_W), lambda i, j: (i, j, 0)),
            pl.BlockSpec(st_block, lambda i, j: (i, 0, 0, 0)),
        ],
        out_shape=[
            jax.ShapeDtypeStruct((b, t, RWKV_W), F32),
            jax.ShapeDtypeStruct((b,) + st_block[1:], F32),
        ],
        scratch_shapes=[
            pltpu.VMEM((g, c + SUBLANES, RWKV_PROJ), F32),
            pltpu.VMEM((g, RWKV_PAIRS, LANES, LANES), F32),
        ],
        compiler_params=pltpu.CompilerParams(
            dimension_semantics=("parallel", "arbitrary"), vmem_limit_bytes=VMEM_LIMIT_BYTES),
        name="rwkv7_mix",
    )(p, prev.reshape(b, 1, RWKV_PROJ), s0, *consts)


def _gdn_kernel(qkv_ref, z_ref, ba_ref, buf_ref, s0_ref, cw_ref, alog_ref, dt_ref, nw_ref,
                out_ref, s_out_ref, xp_ref, state_ref, *, c, g, nc):
    ci = pl.program_id(1)
    base = SUBLANES
    nb = CONV_W - 1
    gc = g * c

    @pl.when(ci == 0)
    def _():
        for s in range(g):
            xp_ref[s, base - nb:base, :] = buf_ref[s]
        state_ref[...] = s0_ref[...]

    qkv = _silu(_cat([_causal_conv(qkv_ref, xp_ref, cw_ref, s, c) for s in range(g)]))
    z = z_ref[...].reshape(gc, GDN_W)
    ba = ba_ref[...].reshape(gc, GDN_BA_PAD)
    beta_all = _sigmoid(ba)
    g_all = -jnp.exp(alog_ref[...]) * _softplus(ba + dt_ref[...])
    n2 = 2 * gc
    incl1, _, same1, _ = _block_masks(gc, c)
    sums = _mm_exact_lhs(jnp.concatenate([incl1, same1], axis=0).astype(BF16), g_all)
    gc_all = sums[:gc]
    tot_all = sums[gc:]
    row = lax.broadcasted_iota(jnp.int32, (gc, n2), 0)
    col = jnp.bitwise_and(lax.broadcasted_iota(jnp.int32, (gc, n2), 1), gc - 1)
    upper = ((jnp.bitwise_xor(row, col) < c) & (row <= col)).astype(BF16)
    dn = (((0,), (0,)), ((), ()))
    gct = sum(lax.dot_general(t, upper, dn, preferred_element_type=F32) for t in _split3(g_all))
    incl, strict, _, diag = _block_masks(n2, c)
    eye = diag.astype(F32)
    first = lax.broadcasted_iota(jnp.int32, (1, n2), 1) < gc

    hs = range(GDN_HEADS)
    ps = range(GDN_HEADS // 2)
    head = lambda x, part, h: x[:, part * GDN_W + h * GDN_HD:part * GDN_W + (h + 1) * GDN_HD]
    q = [head(qkv, 0, h) for h in hs]
    k = [head(qkv, 1, h) for h in hs]
    v = [head(qkv, 2, h) for h in hs]
    q = [x * lax.rsqrt(jnp.sum(x * x, axis=-1, keepdims=True) + 1e-6) * (GDN_HD ** -0.5) for x in q]
    k = [x * lax.rsqrt(jnp.sum(x * x, axis=-1, keepdims=True) + 1e-6) for x in k]
    beta = [beta_all[:, h:h + 1] for h in hs]
    gcol = [gc_all[:, GDN_HEADS + h:GDN_HEADS + h + 1] for h in hs]
    glast = [tot_all[:, GDN_HEADS + h:GDN_HEADS + h + 1] for h in hs]
    kb = [k[h] * beta[h] for h in hs]
    vb = [v[h] * beta[h] for h in hs]
    eg = [jnp.exp(gcol[h]) for h in hs]
    qe = [q[h] * eg[h] for h in hs]
    k_tail = [k[h] * jnp.exp(glast[h] - gcol[h]) for h in hs]

    def bd(x0, x1):
        z0 = jnp.zeros_like(x0)
        return jnp.concatenate([jnp.concatenate([x0, z0], axis=1), jnp.concatenate([z0, x1], axis=1)], axis=0)

    gcol2 = [jnp.concatenate([gcol[2 * j], gcol[2 * j + 1]], axis=0) for j in ps]
    grow2 = [jnp.where(first, gct[GDN_HEADS + 2 * j:GDN_HEADS + 2 * j + 1, :],
                       gct[GDN_HEADS + 2 * j + 1:GDN_HEADS + 2 * j + 2, :]) for j in ps]
    decay = [jnp.where(incl, jnp.exp(jnp.where(incl, gcol2[j] - grow2[j], 0.0)), 0.0) for j in ps]
    gq = [_mm_nt(jnp.concatenate([bd(q[2 * j], q[2 * j + 1]), bd(kb[2 * j], kb[2 * j + 1])], axis=0),
                 bd(k[2 * j], k[2 * j + 1])) for j in ps]
    qk = [jnp.where(incl, gq[j][:n2] * decay[j], 0.0) for j in ps]
    lmat = [jnp.where(strict, gq[j][n2:] * decay[j], 0.0) for j in ps]
    t_inv = _unit_lower_inverses([-x for x in lmat], eye, c)
    rhs = [jnp.concatenate([jnp.concatenate([vb[h], kb[h] * eg[h]], axis=1) for h in (2 * j, 2 * j + 1)], axis=0)
           for j in ps]
    uw = [_mm3(t_inv[j], rhs[j]) for j in ps]

    q_s, w_s = [], []
    for j in ps:
        pieces = []
        for s in range(g):
            w0 = _rows(uw[j][:, GDN_HD:], s, c)
            w1 = _rows(uw[j][:, GDN_HD:], g + s, c)
            lhs = jnp.concatenate([bd(_rows(qe[2 * j], s, c), _rows(qe[2 * j + 1], s, c)), bd(w0, w1)], axis=0)
            s2 = jnp.concatenate([state_ref[s, 2 * j], state_ref[s, 2 * j + 1]], axis=0)
            pieces.append(_mm(lhs, s2))
        q_s.append(_pair_merge([x[:2 * c] for x in pieces], c))
        w_s.append(_pair_merge([x[2 * c:] for x in pieces], c))
    v_new = [uw[j][:, :GDN_HD] - w_s[j] for j in ps]
    o2 = [q_s[j] + _mm(qk[j], v_new[j]) for j in ps]
    for j in ps:
        for s in range(g):
            lk = bd(_rows(k_tail[2 * j], s, c), _rows(k_tail[2 * j + 1], s, c))
            upd = _mm_tn(lk, _pair_rows(v_new[j], s, g, c))
            for i in range(2):
                h = 2 * j + i
                decay_s = jnp.exp(glast[h][s * c:s * c + 1, :])
                state_ref[s, h] = state_ref[s, h] * decay_s + upd[i * GDN_HD:(i + 1) * GDN_HD]
    o = [o2[h // 2][(h % 2) * gc:(h % 2 + 1) * gc] for h in hs]
    outs = []
    for h in hs:
        on = o[h] * lax.rsqrt(jnp.mean(o[h] * o[h], axis=-1, keepdims=True) + NORM_EPS) * nw_ref[...]
        outs.append(on * _silu(z[:, h * GDN_HD:(h + 1) * GDN_HD]))
    out_ref[...] = jnp.concatenate(outs, axis=1).reshape(g, c, GDN_W)

    @pl.when(ci == nc - 1)
    def _():
        s_out_ref[...] = state_ref[...]


def _gdn_mix(qkv, z, ba, buf, s0, prm, c, g):
    b, t, _ = qkv.shape
    nc = t // c
    assert nc * c == t and c % SUBLANES == 0 and b % g == 0
    kern = functools.partial(_gdn_kernel, c=c, g=g, nc=nc)
    consts = [prm["conv_w"], prm["a_log"], prm["dt_bias"], prm["norm_w"]]
    st_block = (g, GDN_HEADS, GDN_HD, GDN_HD)
    return pl.pallas_call(
        kern,
        grid=(b // g, nc),
        in_specs=[
            pl.BlockSpec((g, c, GDN_QKV), lambda i, j: (i, j, 0)),
            pl.BlockSpec((g, c, GDN_W), lambda i, j: (i, j, 0)),
            pl.BlockSpec((g, c, GDN_BA_PAD), lambda i, j: (i, j, 0)),
            pl.BlockSpec((g, CONV_W - 1, GDN_QKV), lambda i, j: (i, 0, 0)),
            pl.BlockSpec(st_block, lambda i, j: (i, 0, 0, 0)),
        ] + [_const_spec(x.shape) for x in consts],
        out_specs=[
            pl.BlockSpec((g, c, GDN_W), lambda i, j: (i, j, 0)),
            pl.BlockSpec(st_block, lambda i, j: (i, 0, 0, 0)),
        ],
        out_shape=[
            jax.ShapeDtypeStruct((b, t, GDN_W), F32),
            jax.ShapeDtypeStruct((b,) + st_block[1:], F32),
        ],
        scratch_shapes=[
            pltpu.VMEM((g, c + SUBLANES, GDN_QKV), F32),
            pltpu.VMEM(st_block, F32),
        ],
        compiler_params=pltpu.CompilerParams(
            dimension_semantics=("parallel", "arbitrary"), vmem_limit_bytes=VMEM_LIMIT_BYTES),
        name="gdn_mix",
    )(qkv, z, ba, buf, s0, *consts)


def _lru_kernel(gate_ref, xb_ref, buf_ref, h0_ref, cw_ref, cb_ref, wab_ref, bab_ref, l_ref,
                out_ref, h_out_ref, xp_ref, carry_ref, sa_ref, sb_ref, *, c, g, nc, pad):
    ci = pl.program_id(1)
    base = SUBLANES
    nb = CONV_W - 1
    gc = g * c

    @pl.when(ci == 0)
    def _():
        for s in range(g):
            xp_ref[s, base - nb:base, :] = buf_ref[s]
        carry_ref[...] = h0_ref[...]
        sa_ref[0:pad, :] = jnp.ones((pad, LRU_W), F32)
        sb_ref[0:pad, :] = jnp.zeros((pad, LRU_W), F32)

    xc = _cat([_causal_conv(xb_ref, xp_ref, cw_ref, s, c) for s in range(g)]) + cb_ref[...]

    for h in range(LRU_HEADS):
        sl = slice(h * LRU_BW, (h + 1) * LRU_BW)
        xh = xc[:, sl]
        ri = _mm(xh, wab_ref[h]) + bab_ref[h]
        r = _sigmoid(ri[:, :LRU_BW])
        i = _sigmoid(ri[:, LRU_BW:])
        log_a = -LRU_C * r * _softplus(-l_ref[:, sl])
        a = jnp.exp(log_a)
        mult = jnp.sqrt(-jnp.tanh(log_a) * (a * a + 1.0))
        sa_ref[pad:pad + gc, sl] = a
        sb_ref[pad:pad + gc, sl] = mult * i * xh

    tpos = jnp.bitwise_and(lax.broadcasted_iota(jnp.int32, (gc, 1), 0), c - 1)
    a = sa_ref[pad:pad + gc, :]
    b = sb_ref[pad:pad + gc, :]
    d = 1
    while d < c:
        live = tpos >= d
        a_sh = jnp.where(live, sa_ref[pad - d:pad - d + gc, :], 1.0)
        b_sh = jnp.where(live, sb_ref[pad - d:pad - d + gc, :], 0.0)
        b = a * b_sh + b
        a = a * a_sh
        d *= 2
        if d < c:
            sa_ref[pad:pad + gc, :] = a
            sb_ref[pad:pad + gc, :] = b
    carry = _cat([jnp.broadcast_to(carry_ref[s], (c, LRU_W)) for s in range(g)])
    hs = a * carry + b
    for s in range(g):
        carry_ref[s] = hs[s * c + c - 1:s * c + c, :]
    out_ref[...] = (hs * _gelu_tanh(gate_ref[...].reshape(gc, LRU_W))).reshape(g, c, LRU_W)

    @pl.when(ci == nc - 1)
    def _():
        h_out_ref[...] = carry_ref[...]


def _lru_mix(gate, xb, buf, h0, prm, c, g):
    b, t, _ = xb.shape
    nc = t // c
    assert nc * c == t and c % SUBLANES == 0 and b % g == 0
    pad = max(c // 2, SUBLANES)
    kern = functools.partial(_lru_kernel, c=c, g=g, nc=nc, pad=pad)
    consts = [prm["conv_w"], prm["conv_b"], prm["wab"], prm["bab"], prm["l"]]
    seq = lambda rows: pl.BlockSpec((g, rows, LRU_W), lambda i, j: (i, 0, 0))
    chunk = pl.BlockSpec((g, c, LRU_W), lambda i, j: (i, j, 0))
    y, h_last = pl.pallas_call(
        kern,
        grid=(b // g, nc),
        in_specs=[chunk, chunk, seq(CONV_W - 1), seq(1)] + [_const_spec(x.shape) for x in consts],
        out_specs=[chunk, seq(1)],
        out_shape=[
            jax.ShapeDtypeStruct((b, t, LRU_W), F32),
            jax.ShapeDtypeStruct((b, 1, LRU_W), F32),
        ],
        scratch_shapes=[
            pltpu.VMEM((g, c + SUBLANES, LRU_W), F32),
            pltpu.VMEM((g, 1, LRU_W), F32),
            pltpu.VMEM((pad + g * c, LRU_W), F32),
            pltpu.VMEM((pad + g * c, LRU_W), F32),
        ],
        compiler_params=pltpu.CompilerParams(
            dimension_semantics=("parallel", "arbitrary"), vmem_limit_bytes=VMEM_LIMIT_BYTES),
        name="lru_mix",
    )(gate, xb, buf, h0.reshape(b, 1, LRU_W), *consts)
    return y, h_last.reshape(b, LRU_W)


def _prep_even(i, w_in_even, w_out_even, rwkv_mu, rwkv_w0, rwkv_w2, rwkv_a0, rwkv_a2, rwkv_g2, rwkv_k_k,
               rwkv_k_a, rwkv_r_k, rwkv_ln_w, rwkv_ln_b, gdn_conv_w, gdn_A_log, gdn_dt_bias, gdn_norm_w):
    w = w_in_even[i]
    o = RWKV_PROJ
    n_ba = 2 * GDN_HEADS
    w_ba = jnp.pad(w[:, o + GDN_QKV + GDN_W:], ((0, 0), (0, GDN_BA_PAD - n_ba)))
    w_cat = jnp.concatenate([w[:, :o + GDN_QKV + GDN_W], w_ba], axis=1).astype(BF16)
    row = lambda x: x.reshape(1, -1)
    lane = jnp.arange(LANES) // RWKV_HD
    lora = DECAY_LORA + AAA_LORA
    rwkv = dict(
        mu=row(rwkv_mu[i]), w0=row(rwkv_w0[i]), a0=row(rwkv_a0[i]),
        w2=jnp.pad(rwkv_w2[i], ((0, lora - DECAY_LORA), (0, 0))).astype(BF16),
        a2=jnp.pad(rwkv_a2[i], ((DECAY_LORA, 0), (0, 0))).astype(BF16),
        g2=rwkv_g2[i].astype(BF16),
        k_k=row(rwkv_k_k[i]), k_a=row(rwkv_k_a[i]), r_k=row(rwkv_r_k[i]),
        ln_w=row(rwkv_ln_w[i]), ln_b=row(rwkv_ln_b[i]),
        hsum=(lane[:, None] == lane[None, :]).astype(BF16),
    )
    pad_heads = lambda x: jnp.pad(x, (GDN_HEADS, GDN_BA_PAD - n_ba)).reshape(1, GDN_BA_PAD)
    gdn = dict(conv_w=gdn_conv_w[i], a_log=pad_heads(gdn_A_log[i]), dt_bias=pad_heads(gdn_dt_bias[i]),
               norm_w=row(gdn_norm_w[i]))
    return w_cat, rwkv, gdn, w_out_even[i].astype(BF16)


def _prep_odd(i, w_in_odd, w_out_odd, lru_conv_w, lru_conv_b, lru_wa, lru_ba, lru_wi, lru_bi, lru_L):
    lru = dict(
        conv_w=lru_conv_w[i], conv_b=lru_conv_b[i].reshape(1, LRU_W),
        wab=jnp.concatenate([lru_wa[i], lru_wi[i]], axis=-1).astype(BF16),
        bab=jnp.concatenate([lru_ba[i], lru_bi[i]], axis=-1).reshape(LRU_HEADS, 1, 2 * LRU_BW),
        l=lru_L[i].reshape(1, LRU_W),
    )
    return w_in_odd[i].astype(BF16), lru, w_out_odd[i].astype(BF16)


def _row_tile(n):
    return 512 if n % 512 == 0 else 256


def _seq_group(b, c, rows):
    g = max(1, min(b, rows // c))
    while b % g:
        g -= 1
    return g


def _trunk(x, st_rwkv, st_shift, st_gdn, st_gconv, st_lru, st_lconv, even_p, odd_p, mlp_p, norm_mix,
           norm_mlp, norm_final):
    b, t, _ = x.shape
    n = b * t
    tm = _row_tile(n)
    c_even = min(64, t)
    g_even = _seq_group(b, c_even, 64)
    c_odd = min(256, t)
    g_odd = _seq_group(b, c_odd, 256)
    h = x.reshape(n, D_MODEL)
    n_rwkv, n_shift, n_gdn, n_gconv, n_lru, n_lconv = [], [], [], [], [], []
    y_final = None
    for l in range(DEPTH):
        i = l // 2
        up, down = mlp_p[l]
        final_nw = norm_final if l == DEPTH - 1 else None
        if l % 2 == 0:
            w_cat, rwkv, gdn, wo = even_p[i]
            p, qkv, z, ba = _norm_proj(h, norm_mix[l], w_cat, (RWKV_PROJ, GDN_QKV, GDN_W, GDN_BA_PAD), tm)
            p = p.reshape(b, t, RWKV_PROJ)
            qkv = qkv.reshape(b, t, GDN_QKV)
            ya, s_a = _rwkv_mix(p, st_shift[i], st_rwkv[i], rwkv, c_even, g_even)
            yb, s_b = _gdn_mix(qkv, z.reshape(b, t, GDN_W), ba.reshape(b, t, GDN_BA_PAD), st_gconv[i],
                               st_gdn[i], gdn, c_even, g_even)
            n_rwkv.append(s_a)
            n_shift.append(p[:, -1])
            n_gdn.append(s_b)
            n_gconv.append(qkv[:, t - (CONV_W - 1):])
            ys = (ya.reshape(n, RWKV_W), yb.reshape(n, GDN_W))
        else:
            w_in, lru, wo = odd_p[i]
            gate, xb = _norm_proj(h, norm_mix[l], w_in, (LRU_W, LRU_W), tm)
            xb = xb.reshape(b, t, LRU_W)
            yc, h_last = _lru_mix(gate.reshape(b, t, LRU_W), xb, st_lconv[i], st_lru[i], lru, c_odd, g_odd)
            n_lru.append(h_last)
            n_lconv.append(xb[:, t - (CONV_W - 1):])
            ys = (yc.reshape(n, LRU_W),)
        outs = _mix_out_mlp(h, ys, wo, norm_mlp[l], up, down, final_nw, tm)
        h = outs[0]
        if final_nw is not None:
            y_final = outs[1]
    return (y_final.reshape(b, t, D_MODEL), jnp.stack(n_rwkv), jnp.stack(n_shift), jnp.stack(n_gdn),
            jnp.stack(n_gconv), jnp.stack(n_lru), jnp.stack(n_lconv))


def kernel(x_prompt, x_sample, state_rwkv, state_rwkv_shift, state_gdn, state_gdn_conv, state_lru, state_lru_conv, norm_mix, norm_mlp, norm_final, w_in_even, w_out_even, rwkv_mu, rwkv_w0, rwkv_w2, rwkv_a0, rwkv_a2, rwkv_g2, rwkv_k_k, rwkv_k_a, rwkv_r_k, rwkv_ln_w, rwkv_ln_b, gdn_conv_w, gdn_A_log, gdn_dt_bias, gdn_norm_w, w_in_odd, w_out_odd, lru_conv_w, lru_conv_b, lru_wa, lru_ba, lru_wi, lru_bi, lru_L, mlp_up, mlp_down):
    n_even = (DEPTH + 1) // 2
    n_odd = DEPTH // 2
    even_p = [_prep_even(i, w_in_even, w_out_even, rwkv_mu, rwkv_w0, rwkv_w2, rwkv_a0, rwkv_a2, rwkv_g2,
                         rwkv_k_k, rwkv_k_a, rwkv_r_k, rwkv_ln_w, rwkv_ln_b, gdn_conv_w, gdn_A_log,
                         gdn_dt_bias, gdn_norm_w) for i in range(n_even)]
    odd_p = [_prep_odd(i, w_in_odd, w_out_odd, lru_conv_w, lru_conv_b, lru_wa, lru_ba, lru_wi, lru_bi, lru_L)
             for i in range(n_odd)]
    mlp_p = [(mlp_up[l].astype(BF16), mlp_down[l].astype(BF16)) for l in range(DEPTH)]
    bp = x_prompt.shape[0]
    dt = x_prompt.dtype
    zeros = lambda *s: jnp.zeros(s, dt)
    shared = (even_p, odd_p, mlp_p, norm_mix, norm_mlp, norm_final)
    y_p, rwkv_p, shift_p, gdn_p, gconv_p, lru_p, lconv_p = _trunk(
        x_prompt,
        zeros(n_even, bp, RWKV_HEADS, RWKV_HD, RWKV_HD), zeros(n_even, bp, RWKV_PROJ),
        zeros(n_even, bp, GDN_HEADS, GDN_HD, GDN_HD), zeros(n_even, bp, CONV_W - 1, GDN_QKV),
        zeros(n_odd, bp, LRU_W), zeros(n_odd, bp, CONV_W - 1, LRU_W), *shared)
    y_s, rwkv_s, shift_s, gdn_s, gconv_s, lru_s, lconv_s = _trunk(
        x_sample, state_rwkv, state_rwkv_shift, state_gdn, state_gdn_conv, state_lru, state_lru_conv, *shared)
    return (y_p, y_s, rwkv_p, rwkv_s, shift_p, shift_s, gdn_p, gdn_s, gconv_p, gconv_s, lru_p, lru_s,
            lconv_p, lconv_s)
```

```python
import functools

import jax
import jax.numpy as jnp
from jax import lax
from jax.experimental import pallas as pl
from jax.experimental.pallas import tpu as pltpu

F32 = jnp.float32
BF16 = jnp.bfloat16

D_MODEL = 1024
DEPTH = 4
CONV_W = 4
NORM_EPS = 1e-6
RWKV_HEADS = 8
RWKV_HD = 64
RWKV_W = RWKV_HEADS * RWKV_HD
RWKV_PAIRS = RWKV_HEADS // 2
DECAY_LORA = 64
AAA_LORA = 64
GATE_LORA = 128
RWKV_PROJ = 3 * RWKV_W + DECAY_LORA + AAA_LORA + GATE_LORA
RWKV_GN_EPS = 64e-5
GDN_HEADS = 4
GDN_HD = 128
GDN_W = GDN_HEADS * GDN_HD
GDN_QKV = 3 * GDN_W
GDN_BA_PAD = 128
LRU_W = D_MODEL
LRU_HEADS = 8
LRU_BW = LRU_W // LRU_HEADS
LRU_C = 8.0
D_FF = 4 * D_MODEL

SUBLANES = 8
LANES = 128
VMEM_LIMIT_BYTES = 56 * 1024 * 1024


def _mm(a, b):
    return jnp.dot(a.astype(BF16), b.astype(BF16), preferred_element_type=F32)


def _mm_nt(a, b):
    return lax.dot_general(a.astype(BF16), b.astype(BF16), (((1,), (1,)), ((), ())),
                           preferred_element_type=F32)


def _mm_tn(a, b):
    return lax.dot_general(a.astype(BF16), b.astype(BF16), (((0,), (0,)), ((), ())),
                           preferred_element_type=F32)


def _split3(x):
    h1 = x.astype(BF16)
    r1 = x - h1.astype(F32)
    h2 = r1.astype(BF16)
    r2 = r1 - h2.astype(F32)
    return h1, h2, r2.astype(BF16)


def _mm_exact_lhs(m01, x):
    h1, h2, h3 = _split3(x)
    d = lambda t: jnp.dot(m01, t, preferred_element_type=F32)
    return d(h1) + d(h2) + d(h3)


def _mm_exact_rhs(x, m01):
    h1, h2, h3 = _split3(x)
    d = lambda t: jnp.dot(t, m01, preferred_element_type=F32)
    return d(h1) + d(h2) + d(h3)


def _mm3(a, b):
    ah = a.astype(BF16)
    al = (a - ah.astype(F32)).astype(BF16)
    bh = b.astype(BF16)
    bl = (b - bh.astype(F32)).astype(BF16)
    d = lambda x, y: jnp.dot(x, y, preferred_element_type=F32)
    return d(ah, bh) + d(ah, bl) + d(al, bh)


def _softplus(x):
    return jnp.maximum(x, 0.0) + jnp.log1p(jnp.exp(-jnp.abs(x)))


def _sigmoid(x):
    return jax.nn.sigmoid(x)


def _silu(x):
    return x * jax.nn.sigmoid(x)


def _gelu_tanh(x):
    return 0.5 * x * (1.0 + jnp.tanh(0.7978845608028654 * (x + 0.044715 * (x * x * x))))


def _block_masks(n, c):
    row = lax.broadcasted_iota(jnp.int32, (n, n), 0)
    col = lax.broadcasted_iota(jnp.int32, (n, n), 1)
    same = jnp.bitwise_xor(row, col) < c
    return same & (row >= col), same & (row > col), same, row == col


def _unit_lower_inverses(ms, eye, c):
    n = eye.shape[0]
    row = lax.broadcasted_iota(jnp.int32, (n, n), 0)
    col = lax.broadcasted_iota(jnp.int32, (n, n), 1)
    corner = lambda s: ((jnp.bitwise_xor(row, col) < 2 * s) & (jnp.bitwise_and(row, s) != 0)
                        & (jnp.bitwise_and(col, s) == 0))
    ps = [eye + jnp.where(corner(1), m, 0.0) for m in ms]
    s = 2
    while s < c:
        mask = corner(s)
        pb = [_mm(p, jnp.where(mask, m, 0.0)) for p, m in zip(ps, ms)]
        ps = [p + _mm(x, p) for p, x in zip(ps, pb)]
        s *= 2
    return ps


def _rows(x, s, c):
    return x[s * c:(s + 1) * c]


def _cat(xs, axis=0):
    return xs[0] if len(xs) == 1 else jnp.concatenate(xs, axis=axis)


def _const_spec(shape):
    nd = len(shape)
    return pl.BlockSpec(shape, lambda *_: (0,) * nd)


def _causal_conv(x_ref, xp_ref, cw_ref, s, c):
    base = SUBLANES
    nb = CONV_W - 1
    xp_ref[s, base:base + c, :] = x_ref[s]
    y = cw_ref[0:1, :] * xp_ref[s, base - nb:base - nb + c, :]
    for j in range(1, CONV_W):
        y = y + cw_ref[j:j + 1, :] * xp_ref[s, base - nb + j:base - nb + j + c, :]
    xp_ref[s, base - nb:base, :] = xp_ref[s, base + c - nb:base + c, :]
    return y


def _norm_proj_kernel(x_ref, nw_ref, *refs, groups, col_chunk):
    n_w = len(groups)
    w_refs, o_refs = refs[:n_w], refs[n_w:]
    x = x_ref[...]
    var = jnp.mean(x * x, axis=-1, keepdims=True)
    xn = (x * lax.rsqrt(var + NORM_EPS) * nw_ref[...]).astype(BF16)
    o_iter = iter(o_refs)
    for w_ref, widths in zip(w_refs, groups):
        off = 0
        for width in widths:
            o_ref = next(o_iter)
            for c0 in range(0, width, col_chunk):
                cw = min(col_chunk, width - c0)
                o_ref[:, c0:c0 + cw] = jnp.dot(xn, w_ref[:, off + c0:off + c0 + cw],
                                               preferred_element_type=F32)
            off += width


def _norm_proj(x, nw, ws, layer, groups, tm):
    n = x.shape[0]
    assert n % tm == 0 and all(sum(gr) == w.shape[2] for gr, w in zip(groups, ws))
    kern = functools.partial(_norm_proj_kernel, groups=tuple(tuple(gr) for gr in groups), col_chunk=512)
    widths = [wd for gr in groups for wd in gr]
    return pl.pallas_call(
        kern,
        grid=(n // tm,),
        in_specs=[
            pl.BlockSpec((tm, D_MODEL), lambda i: (i, 0)),
            _const_spec((1, D_MODEL)),
        ] + [pl.BlockSpec((None, D_MODEL, w.shape[2]), lambda i: (layer, 0, 0)) for w in ws],
        out_specs=[pl.BlockSpec((tm, wd), lambda i: (i, 0)) for wd in widths],
        out_shape=[jax.ShapeDtypeStruct((n, wd), F32) for wd in widths],
        compiler_params=pltpu.CompilerParams(
            dimension_semantics=("parallel",), vmem_limit_bytes=VMEM_LIMIT_BYTES),
        name="norm_proj",
    )(x, nw.reshape(1, D_MODEL), *ws)


def _mlp_kernel(*refs, n_mix, ff_chunk, final):
    h_ref = refs[0]
    y_refs = refs[1:1 + n_mix]
    wo_ref, nw_ref, up_ref, down_ref = refs[1 + n_mix:5 + n_mix]
    rest = refs[5 + n_mix:]
    if final:
        fnw_ref, o_ref, yf_ref = rest
    else:
        (o_ref,) = rest
    y = _cat([y_ref[...].astype(BF16) for y_ref in y_refs], axis=1)
    h = h_ref[...] + jnp.dot(y, wo_ref[...], preferred_element_type=F32)
    var = jnp.mean(h * h, axis=-1, keepdims=True)
    xn = (h * lax.rsqrt(var + NORM_EPS) * nw_ref[...]).astype(BF16)
    acc = h
    for c0 in range(0, D_FF, ff_chunk):
        u = jnp.dot(xn, up_ref[:, c0:c0 + ff_chunk], preferred_element_type=F32)
        r = jnp.square(jnp.maximum(u, 0.0)).astype(BF16)
        acc = acc + jnp.dot(r, down_ref[c0:c0 + ff_chunk, :], preferred_element_type=F32)
    o_ref[...] = acc
    if final:
        var2 = jnp.mean(acc * acc, axis=-1, keepdims=True)
        yf_ref[...] = acc * lax.rsqrt(var2 + NORM_EPS) * fnw_ref[...]


def _mix_out_mlp(h, ys, wo, nw, up, down, final_nw, tm):
    n = h.shape[0]
    n_mix = len(ys)
    final = final_nw is not None
    kern = functools.partial(_mlp_kernel, n_mix=n_mix, ff_chunk=512, final=final)
    row = lambda wd: pl.BlockSpec((tm, wd), lambda i: (i, 0))
    in_specs = [row(D_MODEL)] + [row(y.shape[1]) for y in ys]
    in_specs += [_const_spec(wo.shape), _const_spec((1, D_MODEL)), _const_spec((D_MODEL, D_FF)),
                 _const_spec((D_FF, D_MODEL))]
    args = [h, *ys, wo, nw.reshape(1, D_MODEL), up, down]
    out_specs = [row(D_MODEL)]
    out_shape = [jax.ShapeDtypeStruct((n, D_MODEL), F32)]
    if final:
        in_specs.append(_const_spec((1, D_MODEL)))
        args.append(final_nw.reshape(1, D_MODEL))
        out_specs.append(row(D_MODEL))
        out_shape.append(jax.ShapeDtypeStruct((n, D_MODEL), F32))
    return pl.pallas_call(
        kern,
        grid=(n // tm,),
        in_specs=in_specs,
        out_specs=out_specs,
        out_shape=out_shape,
        compiler_params=pltpu.CompilerParams(
            dimension_semantics=("parallel",), vmem_limit_bytes=VMEM_LIMIT_BYTES),
        name="mix_out_mlp",
    )(*args)


def _pair_rows(x, s, g, c):
    if g == 1:
        return x
    return jnp.concatenate([_rows(x, s, c), _rows(x, g + s, c)], axis=0)


def _pair_merge(pieces, c):
    return _cat([p[:c] for p in pieces] + [p[c:] for p in pieces])


def _rwkv_kernel(p_ref, prev_ref, s0_ref, mu_ref, w0_ref, w2_ref, a0_ref, a2_ref, g2_ref,
                 kk_ref, ka_ref, rk_ref, lnw_ref, lnb_ref, hsum_ref,
                 out_ref, s_out_ref, xp_ref, state_ref, *, c, g, nc):
    ci = pl.program_id(1)
    base = SUBLANES
    gc = g * c

    hd = RWKV_HD

    @pl.when(ci == 0)
    def _():
        for s in range(g):
            xp_ref[s, base - 1:base, :] = prev_ref[s]
        state_ref[...] = jnp.zeros(state_ref.shape, F32)
        for s in range(g):
            for j in range(RWKV_PAIRS):
                state_ref[s, j, 0:hd, 0:hd] = s0_ref[s, 2 * j]
                state_ref[s, j, hd:2 * hd, hd:2 * hd] = s0_ref[s, 2 * j + 1]

    shifted = []
    for s in range(g):
        ps = p_ref[s]
        xp_ref[s, base:base + c, :] = ps
        shifted.append(xp_ref[s, base - 1:base - 1 + c, :])
        xp_ref[s, base - 1:base, :] = ps[c - 1:c, :]
    p = p_ref[...].reshape(gc, RWKV_PROJ)
    shifted = _cat(shifted)

    xs = p + mu_ref[...] * (shifted - p)
    r = xs[:, 0:RWKV_W]
    k = xs[:, RWKV_W:2 * RWKV_W]
    v = xs[:, 2 * RWKV_W:3 * RWKV_W]
    xwa = xs[:, 3 * RWKV_W:3 * RWKV_W + DECAY_LORA + AAA_LORA]
    xg = xs[:, 3 * RWKV_W + DECAY_LORA + AAA_LORA:]

    w_log = -_softplus(-(w0_ref[...] + _mm(jnp.tanh(xwa), w2_ref[...]))) - 0.5
    ld = -jnp.exp(w_log)
    a = _sigmoid(a0_ref[...] + _mm(xwa, a2_ref[...]))
    gate = _mm(_sigmoid(xg), g2_ref[...])

    incl1, _, same1, _ = _block_masks(gc, c)
    sums = _mm_exact_lhs(jnp.concatenate([incl1, same1], axis=0).astype(BF16), ld)
    cum = sums[:gc]
    tot = sums[gc:]
    w_in = jnp.exp(cum)
    w_ex = jnp.exp(cum - ld)
    w_inv = jnp.exp(-cum)
    w_tail = jnp.exp(tot - cum)
    w_last = jnp.exp(tot)

    hsum = hsum_ref[...]

    def head_sums(x):
        return jnp.concatenate(
            [_mm_exact_rhs(x[:, j * LANES:(j + 1) * LANES], hsum) for j in range(RWKV_PAIRS)], axis=1)

    kkr = k * kk_ref[...]
    kk = kkr * lax.rsqrt(head_sums(kkr * kkr) + 1e-6)
    kp = k * (1.0 + (a - 1.0) * ka_ref[...])
    bv = kk * a

    lo = lax.broadcasted_iota(jnp.int32, (gc, LANES), 1) < RWKV_HD

    def pairs(x):
        out = []
        for j in range(RWKV_PAIRS):
            slab = x[:, j * LANES:(j + 1) * LANES]
            out.append(jnp.concatenate([jnp.where(lo, slab, 0.0), jnp.where(lo, 0.0, slab)], axis=0))
        return out

    a_t = pairs(-kk * w_ex)
    r_t = pairs(r * w_in)
    b_t = pairs(bv * w_inv)
    k_t = pairs(kp * w_inv)
    v_b = pairs(v)
    b_w = pairs(bv * w_tail)
    k_w = pairs(kp * w_tail)

    n2 = 2 * gc
    incl, strict, _, diag = _block_masks(n2, c)
    eye = diag.astype(F32)
    ars = [jnp.concatenate([a_t[j], r_t[j]], axis=0) for j in range(RWKV_PAIRS)]
    gbs = [_mm_nt(ars[j], b_t[j]) for j in range(RWKV_PAIRS)]
    gks = [_mm_nt(ars[j], k_t[j]) for j in range(RWKV_PAIRS)]
    m_ab = [jnp.where(strict, x[:n2], 0.0) for x in gbs]
    m_ak = [jnp.where(strict, x[:n2], 0.0) for x in gks]
    l_rb = [jnp.where(incl, x[n2:], 0.0) for x in gbs]
    l_rk = [jnp.where(incl, x[n2:], 0.0) for x in gks]
    t_inv = _unit_lower_inverses(m_ab, eye, c)

    sa, sr = [], []
    for j in range(RWKV_PAIRS):
        pieces = []
        for s in range(g):
            lhs = jnp.concatenate([_pair_rows(a_t[j], s, g, c), _pair_rows(r_t[j], s, g, c)], axis=0)
            pieces.append(_mm_nt(lhs, state_ref[s, j]))
        sa.append(_pair_merge([x[:2 * c] for x in pieces], c))
        sr.append(_pair_merge([x[2 * c:] for x in pieces], c))

    mv = [_mm(m_ak[j], v_b[j]) for j in range(RWKV_PAIRS)]
    us = [_mm3(t_inv[j], sa[j] + mv[j]) for j in range(RWKV_PAIRS)]
    ys = [sr[j] + _mm(l_rb[j], us[j]) + _mm(l_rk[j], v_b[j]) for j in range(RWKV_PAIRS)]
    y = jnp.concatenate([x[:gc] + x[gc:] for x in ys], axis=1)

    for j in range(RWKV_PAIRS):
        for s in range(g):
            uv = jnp.concatenate([_pair_rows(us[j], s, g, c), _pair_rows(v_b[j], s, g, c)], axis=0)
            bk = jnp.concatenate([_pair_rows(b_w[j], s, g, c), _pair_rows(k_w[j], s, g, c)], axis=0)
            decay = w_last[s * c:s * c + 1, j * LANES:(j + 1) * LANES]
            state_ref[s, j] = state_ref[s, j] * decay + _mm_tn(uv, bk)

    inv_hd = 1.0 / RWKV_HD
    sums = head_sums(jnp.concatenate([y, y * y, r * kp * rk_ref[...]], axis=0))
    mean = sums[:gc] * inv_hd
    var = sums[gc:2 * gc] * inv_hd - mean * mean
    yn = (y - mean) * lax.rsqrt(var + RWKV_GN_EPS) * lnw_ref[...] + lnb_ref[...]
    out_ref[...] = ((yn + sums[2 * gc:] * v) * gate).reshape(g, c, RWKV_W)

    @pl.when(ci == nc - 1)
    def _():
        for s in range(g):
            for j in range(RWKV_PAIRS):
                s_out_ref[s, 2 * j] = state_ref[s, j, 0:hd, 0:hd]
                s_out_ref[s, 2 * j + 1] = state_ref[s, j, hd:2 * hd, hd:2 * hd]


def _rwkv_mix(p, prev, s0, prm, c, g):
    b, t, _ = p.shape
    nc = t // c
    assert nc * c == t and c % SUBLANES == 0 and b % g == 0
    kern = functools.partial(_rwkv_kernel, c=c, g=g, nc=nc)
    consts = [prm["mu"], prm["w0"], prm["w2"], prm["a0"], prm["a2"], prm["g2"], prm["k_k"], prm["k_a"],
              prm["r_k"], prm["ln_w"], prm["ln_b"], prm["hsum"]]
    st_block = (g, RWKV_HEADS, RWKV_HD, RWKV_HD)
    return pl.pallas_call(
        kern,
        grid=(b // g, nc),
        in_specs=[
            pl.BlockSpec((g, c, RWKV_PROJ), lambda i, j: (i, j, 0)),
            pl.BlockSpec((g, 1, RWKV_PROJ), lambda i, j: (i, 0, 0)),
            pl.BlockSpec(st_block, lambda i, j: (i, 0, 0, 0)),
        ] + [_const_spec(x.shape) for x in consts],
        out_specs=[
            pl.BlockSpec((g, c, RWKV_W), lambda i, j: (i, j, 0)),
            pl.BlockSpec(st_block, lambda i, j: (i, 0, 0, 0)),
        ],
        out_shape=[
            jax.ShapeDtypeStruct((b, t, RWKV_W), F32),
            jax.ShapeDtypeStruct((b,) + st_block[1:], F32),
        ],
        scratch_shapes=[
            pltpu.VMEM((g, c + SUBLANES, RWKV_PROJ), F32),
            pltpu.VMEM((g, RWKV_PAIRS, LANES, LANES), F32),
        ],
        compiler_params=pltpu.CompilerParams(
            dimension_semantics=("parallel", "arbitrary"), vmem_limit_bytes=VMEM_LIMIT_BYTES),
        name="rwkv7_mix",
    )(p, prev.reshape(b, 1, RWKV_PROJ), s0, *consts)


def _gdn_kernel(qkv_ref, z_ref, ba_ref, buf_ref, s0_ref, cw_ref, alog_ref, dt_ref, nw_ref,
                out_ref, s_out_ref, xp_ref, state_ref, *, c, g, nc):
    ci = pl.program_id(1)
    base = SUBLANES
    nb = CONV_W - 1
    gc = g * c

    @pl.when(ci == 0)
    def _():
        for s in range(g):
            xp_ref[s, base - nb:base, :] = buf_ref[s]
        state_ref[...] = s0_ref[...]

    qkv = _silu(_cat([_causal_conv(qkv_ref, xp_ref, cw_ref, s, c) for s in range(g)]))
    z = z_ref[...].reshape(gc, GDN_W)
    ba = ba_ref[...].reshape(gc, GDN_BA_PAD)
    beta_all = _sigmoid(ba)
    g_all = -jnp.exp(alog_ref[...]) * _softplus(ba + dt_ref[...])
    n2 = 2 * gc
    incl1, _, same1, _ = _block_masks(gc, c)
    sums = _mm_exact_lhs(jnp.concatenate([incl1, same1], axis=0).astype(BF16), g_all)
    gc_all = sums[:gc]
    tot_all = sums[gc:]
    row = lax.broadcasted_iota(jnp.int32, (gc, n2), 0)
    col = jnp.bitwise_and(lax.broadcasted_iota(jnp.int32, (gc, n2), 1), gc - 1)
    upper = ((jnp.bitwise_xor(row, col) < c) & (row <= col)).astype(BF16)
    dn = (((0,), (0,)), ((), ()))
    gct = sum(lax.dot_general(t, upper, dn, preferred_element_type=F32) for t in _split3(g_all))
    incl, strict, _, diag = _block_masks(n2, c)
    eye = diag.astype(F32)
    first = lax.broadcasted_iota(jnp.int32, (1, n2), 1) < gc

    hs = range(GDN_HEADS)
    ps = range(GDN_HEADS // 2)
    head = lambda x, part, h: x[:, part * GDN_W + h * GDN_HD:part * GDN_W + (h + 1) * GDN_HD]
    q = [head(qkv, 0, h) for h in hs]
    k = [head(qkv, 1, h) for h in hs]
    v = [head(qkv, 2, h) for h in hs]
    q = [x * lax.rsqrt(jnp.sum(x * x, axis=-1, keepdims=True) + 1e-6) * (GDN_HD ** -0.5) for x in q]
    k = [x * lax.rsqrt(jnp.sum(x * x, axis=-1, keepdims=True) + 1e-6) for x in k]
    beta = [beta_all[:, h:h + 1] for h in hs]
    gcol = [gc_all[:, GDN_HEADS + h:GDN_HEADS + h + 1] for h in hs]
    glast = [tot_all[:, GDN_HEADS + h:GDN_HEADS + h + 1] for h in hs]
    kb = [k[h] * beta[h] for h in hs]
    vb = [v[h] * beta[h] for h in hs]
    eg = [jnp.exp(gcol[h]) for h in hs]
    qe = [q[h] * eg[h] for h in hs]
    k_tail = [k[h] * jnp.exp(glast[h] - gcol[h]) for h in hs]

    def bd(x0, x1):
        z0 = jnp.zeros_like(x0)
        return jnp.concatenate([jnp.concatenate([x0, z0], axis=1), jnp.concatenate([z0, x1], axis=1)], axis=0)

    gcol2 = [jnp.concatenate([gcol[2 * j], gcol[2 * j + 1]], axis=0) for j in ps]
    grow2 = [jnp.where(first, gct[GDN_HEADS + 2 * j:GDN_HEADS + 2 * j + 1, :],
                       gct[GDN_HEADS + 2 * j + 1:GDN_HEADS + 2 * j + 2, :]) for j in ps]
    decay = [jnp.where(incl, jnp.exp(jnp.where(incl, gcol2[j] - grow2[j], 0.0)), 0.0) for j in ps]
    gq = [_mm_nt(jnp.concatenate([bd(q[2 * j], q[2 * j + 1]), bd(kb[2 * j], kb[2 * j + 1])], axis=0),
                 bd(k[2 * j], k[2 * j + 1])) for j in ps]
    qk = [jnp.where(incl, gq[j][:n2] * decay[j], 0.0) for j in ps]
    lmat = [jnp.where(strict, gq[j][n2:] * decay[j], 0.0) for j in ps]
    t_inv = _unit_lower_inverses([-x for x in lmat], eye, c)
    rhs = [jnp.concatenate([jnp.concatenate([vb[h], kb[h] * eg[h]], axis=1) for h in (2 * j, 2 * j + 1)], axis=0)
           for j in ps]
    uw = [_mm3(t_inv[j], rhs[j]) for j in ps]

    q_s, w_s = [], []
    for j in ps:
        pieces = []
        for s in range(g):
            w0 = _rows(uw[j][:, GDN_HD:], s, c)
            w1 = _rows(uw[j][:, GDN_HD:], g + s, c)
            lhs = jnp.concatenate([bd(_rows(qe[2 * j], s, c), _rows(qe[2 * j + 1], s, c)), bd(w0, w1)], axis=0)
            s2 = jnp.concatenate([state_ref[s, 2 * j], state_ref[s, 2 * j + 1]], axis=0)
            pieces.append(_mm(lhs, s2))
        q_s.append(_pair_merge([x[:2 * c] for x in pieces], c))
        w_s.append(_pair_merge([x[2 * c:] for x in pieces], c))
    v_new = [uw[j][:, :GDN_HD] - w_s[j] for j in ps]
    o2 = [q_s[j] + _mm(qk[j], v_new[j]) for j in ps]
    for j in ps:
        for s in range(g):
            lk = bd(_rows(k_tail[2 * j], s, c), _rows(k_tail[2 * j + 1], s, c))
            upd = _mm_tn(lk, _pair_rows(v_new[j], s, g, c))
            for i in range(2):
                h = 2 * j + i
                decay_s = jnp.exp(glast[h][s * c:s * c + 1, :])
                state_ref[s, h] = state_ref[s, h] * decay_s + upd[i * GDN_HD:(i + 1) * GDN_HD]
    o = [o2[h // 2][(h % 2) * gc:(h % 2 + 1) * gc] for h in hs]
    outs = []
    for h in hs:
        on = o[h] * lax.rsqrt(jnp.mean(o[h] * o[h], axis=-1, keepdims=True) + NORM_EPS) * nw_ref[...]
        outs.append(on * _silu(z[:, h * GDN_HD:(h + 1) * GDN_HD]))
    out_ref[...] = jnp.concatenate(outs, axis=1).reshape(g, c, GDN_W)

    @pl.when(ci == nc - 1)
    def _():
        s_out_ref[...] = state_ref[...]


def _gdn_mix(qkv, z, ba, buf, s0, prm, c, g):
    b, t, _ = qkv.shape
    nc = t // c
    assert nc * c == t and c % SUBLANES == 0 and b % g == 0
    kern = functools.partial(_gdn_kernel, c=c, g=g, nc=nc)
    consts = [prm["conv_w"], prm["a_log"], prm["dt_bias"], prm["norm_w"]]
    st_block = (g, GDN_HEADS, GDN_HD, GDN_HD)
    return pl.pallas_call(
        kern,
        grid=(b // g, nc),
        in_specs=[
            pl.BlockSpec((g, c, GDN_QKV), lambda i, j: (i, j, 0)),
            pl.BlockSpec((g, c, GDN_W), lambda i, j: (i, j, 0)),
            pl.BlockSpec((g, c, GDN_BA_PAD), lambda i, j: (i, j, 0)),
            pl.BlockSpec((g, CONV_W - 1, GDN_QKV), lambda i, j: (i, 0, 0)),
            pl.BlockSpec(st_block, lambda i, j: (i, 0, 0, 0)),
        ] + [_const_spec(x.shape) for x in consts],
        out_specs=[
            pl.BlockSpec((g, c, GDN_W), lambda i, j: (i, j, 0)),
            pl.BlockSpec(st_block, lambda i, j: (i, 0, 0, 0)),
        ],
        out_shape=[
            jax.ShapeDtypeStruct((b, t, GDN_W), F32),
            jax.ShapeDtypeStruct((b,) + st_block[1:], F32),
        ],
        scratch_shapes=[
            pltpu.VMEM((g, c + SUBLANES, GDN_QKV), F32),
            pltpu.VMEM(st_block, F32),
        ],
        compiler_params=pltpu.CompilerParams(
            dimension_semantics=("parallel", "arbitrary"), vmem_limit_bytes=VMEM_LIMIT_BYTES),
        name="gdn_mix",
    )(qkv, z, ba, buf, s0, *consts)


def _lru_kernel(gate_ref, xb_ref, buf_ref, h0_ref, cw_ref, cb_ref, wab_ref, bab_ref, l_ref,
                out_ref, h_out_ref, xp_ref, carry_ref, sa_ref, sb_ref, *, c, g, nc):
    ci = pl.program_id(1)
    base = SUBLANES
    nb = CONV_W - 1
    gc = g * c

    @pl.when(ci == 0)
    def _():
        for s in range(g):
            xp_ref[s, base - nb:base, :] = buf_ref[s]
        carry_ref[...] = h0_ref[...]

    xc = _cat([_causal_conv(xb_ref, xp_ref, cw_ref, s, c) for s in range(g)]) + cb_ref[...]

    for h in range(LRU_HEADS):
        sl = slice(h * LRU_BW, (h + 1) * LRU_BW)
        xh = xc[:, sl]
        ri = _mm(xh, wab_ref[h]) + bab_ref[h]
        r = _sigmoid(ri[:, :LRU_BW])
        i = _sigmoid(ri[:, LRU_BW:])
        log_a = -LRU_C * r * _softplus(-l_ref[:, sl])
        a = jnp.exp(log_a)
        mult = jnp.sqrt(-jnp.tanh(log_a) * (a * a + 1.0))
        sa_ref[:, sl] = a
        sb_ref[:, sl] = mult * i * xh

    tpos = jnp.bitwise_and(lax.broadcasted_iota(jnp.int32, (gc, 1), 0), SUBLANES - 1)
    a = sa_ref[...]
    b = sb_ref[...]

    def shift_in_group(x, d):
        x3 = x.reshape(gc // SUBLANES, SUBLANES, LRU_W)
        return pltpu.roll(x3, d, axis=1).reshape(gc, LRU_W)

    d = 1
    while d < SUBLANES:
        live = tpos >= d
        a_sh = jnp.where(live, shift_in_group(a, d), 1.0)
        b_sh = jnp.where(live, shift_in_group(b, d), 0.0)
        b = a * b_sh + b
        a = a * a_sh
        d *= 2
    groups = []
    for s in range(g):
        h = carry_ref[s]
        for r0 in range(s * c, (s + 1) * c, SUBLANES):
            hg = a[r0:r0 + SUBLANES] * h + b[r0:r0 + SUBLANES]
            groups.append(hg)
            h = hg[SUBLANES - 1:SUBLANES]
        carry_ref[s] = h
    hs = _cat(groups)
    out_ref[...] = (hs * _gelu_tanh(gate_ref[...].reshape(gc, LRU_W))).reshape(g, c, LRU_W)

    @pl.when(ci == nc - 1)
    def _():
        h_out_ref[...] = carry_ref[...]


def _lru_mix(gate, xb, buf, h0, prm, c, g):
    b, t, _ = xb.shape
    nc = t // c
    assert nc * c == t and c % SUBLANES == 0 and b % g == 0
    kern = functools.partial(_lru_kernel, c=c, g=g, nc=nc)
    consts = [prm["conv_w"], prm["conv_b"], prm["wab"], prm["bab"], prm["l"]]
    seq = lambda rows: pl.BlockSpec((g, rows, LRU_W), lambda i, j: (i, 0, 0))
    chunk = pl.BlockSpec((g, c, LRU_W), lambda i, j: (i, j, 0))
    y, h_last = pl.pallas_call(
        kern,
        grid=(b // g, nc),
        in_specs=[chunk, chunk, seq(CONV_W - 1), seq(1)] + [_const_spec(x.shape) for x in consts],
        out_specs=[chunk, seq(1)],
        out_shape=[
            jax.ShapeDtypeStruct((b, t, LRU_W), F32),
            jax.ShapeDtypeStruct((b, 1, LRU_W), F32),
        ],
        scratch_shapes=[
            pltpu.VMEM((g, c + SUBLANES, LRU_W), F32),
            pltpu.VMEM((g, 1, LRU_W), F32),
            pltpu.VMEM((g * c, LRU_W), F32),
            pltpu.VMEM((g * c, LRU_W), F32),
        ],
        compiler_params=pltpu.CompilerParams(
            dimension_semantics=("parallel", "arbitrary"), vmem_limit_bytes=VMEM_LIMIT_BYTES),
        name="lru_mix",
    )(gate, xb, buf, h0.reshape(b, 1, LRU_W), *consts)
    return y, h_last.reshape(b, LRU_W)


def _prep_even(i, w_out_even, rwkv_mu, rwkv_w0, rwkv_w2, rwkv_a0, rwkv_a2, rwkv_g2, rwkv_k_k,
               rwkv_k_a, rwkv_r_k, rwkv_ln_w, rwkv_ln_b, gdn_conv_w, gdn_A_log, gdn_dt_bias, gdn_norm_w):
    n_ba = 2 * GDN_HEADS
    row = lambda x: x.reshape(1, -1)
    lane = jnp.arange(LANES) // RWKV_HD
    lora = DECAY_LORA + AAA_LORA
    rwkv = dict(
        mu=row(rwkv_mu[i]), w0=row(rwkv_w0[i]), a0=row(rwkv_a0[i]),
        w2=jnp.pad(rwkv_w2[i], ((0, lora - DECAY_LORA), (0, 0))).astype(BF16),
        a2=jnp.pad(rwkv_a2[i], ((DECAY_LORA, 0), (0, 0))).astype(BF16),
        g2=rwkv_g2[i].astype(BF16),
        k_k=row(rwkv_k_k[i]), k_a=row(rwkv_k_a[i]), r_k=row(rwkv_r_k[i]),
        ln_w=row(rwkv_ln_w[i]), ln_b=row(rwkv_ln_b[i]),
        hsum=(lane[:, None] == lane[None, :]).astype(BF16),
    )
    pad_heads = lambda x: jnp.pad(x, (GDN_HEADS, GDN_BA_PAD - n_ba)).reshape(1, GDN_BA_PAD)
    gdn = dict(conv_w=gdn_conv_w[i], a_log=pad_heads(gdn_A_log[i]), dt_bias=pad_heads(gdn_dt_bias[i]),
               norm_w=row(gdn_norm_w[i]))
    return rwkv, gdn, w_out_even[i].astype(BF16)


def _prep_odd(i, w_out_odd, lru_conv_w, lru_conv_b, lru_wa, lru_ba, lru_wi, lru_bi, lru_L):
    lru = dict(
        conv_w=lru_conv_w[i], conv_b=lru_conv_b[i].reshape(1, LRU_W),
        wab=jnp.concatenate([lru_wa[i], lru_wi[i]], axis=-1).astype(BF16),
        bab=jnp.concatenate([lru_ba[i], lru_bi[i]], axis=-1).reshape(LRU_HEADS, 1, 2 * LRU_BW),
        l=lru_L[i].reshape(1, LRU_W),
    )
    return lru, w_out_odd[i].astype(BF16)


def _prep_in_proj(w_in_even, w_in_odd):
    main = RWKV_PROJ + GDN_QKV + GDN_W
    n_ba = 2 * GDN_HEADS
    even_ba = jnp.pad(w_in_even[:, :, main:], ((0, 0), (0, 0), (0, GDN_BA_PAD - n_ba)))
    return w_in_even[:, :, :main].astype(BF16), even_ba.astype(BF16), w_in_odd.astype(BF16)


def _row_tile(n):
    return 512 if n % 512 == 0 else 256


def _seq_group(b, c, rows):
    g = max(1, min(b, rows // c))
    while b % g:
        g -= 1
    return g


def _trunk(x, st_rwkv, st_shift, st_gdn, st_gconv, st_lru, st_lconv, in_w, even_p, odd_p, mlp_p, norm_mix,
           norm_mlp, norm_final):
    b, t, _ = x.shape
    n = b * t
    tm = _row_tile(n)
    c_even = min(64, t)
    g_rwkv = _seq_group(b, c_even, 128)
    g_gdn = _seq_group(b, c_even, 128)
    c_odd = min(256, t)
    g_odd = _seq_group(b, c_odd, 256)
    h = x.reshape(n, D_MODEL)
    n_rwkv, n_shift, n_gdn, n_gconv, n_lru, n_lconv = [], [], [], [], [], []
    y_final = None
    for l in range(DEPTH):
        i = l // 2
        up, down = mlp_p[l]
        final_nw = norm_final if l == DEPTH - 1 else None
        if l % 2 == 0:
            rwkv, gdn, wo = even_p[i]
            p, qkv, z, ba = _norm_proj(h, norm_mix[l], in_w[:2], i,
                                       ((RWKV_PROJ, GDN_QKV, GDN_W), (GDN_BA_PAD,)), tm)
            p = p.reshape(b, t, RWKV_PROJ)
            qkv = qkv.reshape(b, t, GDN_QKV)
            ya, s_a = _rwkv_mix(p, st_shift[i], st_rwkv[i], rwkv, c_even, g_rwkv)
            yb, s_b = _gdn_mix(qkv, z.reshape(b, t, GDN_W), ba.reshape(b, t, GDN_BA_PAD), st_gconv[i],
                               st_gdn[i], gdn, c_even, g_gdn)
            n_rwkv.append(s_a)
            n_shift.append(p[:, -1])
            n_gdn.append(s_b)
            n_gconv.append(qkv[:, t - (CONV_W - 1):])
            ys = (ya.reshape(n, RWKV_W), yb.reshape(n, GDN_W))
        else:
            lru, wo = odd_p[i]
            gate, xb = _norm_proj(h, norm_mix[l], in_w[2:], i, ((LRU_W, LRU_W),), tm)
            xb = xb.reshape(b, t, LRU_W)
            yc, h_last = _lru_mix(gate.reshape(b, t, LRU_W), xb, st_lconv[i], st_lru[i], lru, c_odd, g_odd)
            n_lru.append(h_last)
            n_lconv.append(xb[:, t - (CONV_W - 1):])
            ys = (yc.reshape(n, LRU_W),)
        outs = _mix_out_mlp(h, ys, wo, norm_mlp[l], up, down, final_nw, tm)
        h = outs[0]
        if final_nw is not None:
            y_final = outs[1]
    return (y_final.reshape(b, t, D_MODEL), jnp.stack(n_rwkv), jnp.stack(n_shift), jnp.stack(n_gdn),
            jnp.stack(n_gconv), jnp.stack(n_lru), jnp.stack(n_lconv))


def kernel(x_prompt, x_sample, state_rwkv, state_rwkv_shift, state_gdn, state_gdn_conv, state_lru, state_lru_conv, norm_mix, norm_mlp, norm_final, w_in_even, w_out_even, rwkv_mu, rwkv_w0, rwkv_w2, rwkv_a0, rwkv_a2, rwkv_g2, rwkv_k_k, rwkv_k_a, rwkv_r_k, rwkv_ln_w, rwkv_ln_b, gdn_conv_w, gdn_A_log, gdn_dt_bias, gdn_norm_w, w_in_odd, w_out_odd, lru_conv_w, lru_conv_b, lru_wa, lru_ba, lru_wi, lru_bi, lru_L, mlp_up, mlp_down):
    n_even = (DEPTH + 1) // 2
    n_odd = DEPTH // 2
    even_p = [_prep_even(i, w_out_even, rwkv_mu, rwkv_w0, rwkv_w2, rwkv_a0, rwkv_a2, rwkv_g2,
                         rwkv_k_k, rwkv_k_a, rwkv_r_k, rwkv_ln_w, rwkv_ln_b, gdn_conv_w, gdn_A_log,
                         gdn_dt_bias, gdn_norm_w) for i in range(n_even)]
    odd_p = [_prep_odd(i, w_out_odd, lru_conv_w, lru_conv_b, lru_wa, lru_ba, lru_wi, lru_bi, lru_L)
             for i in range(n_odd)]
    mlp_p = [(mlp_up[l].astype(BF16), mlp_down[l].astype(BF16)) for l in range(DEPTH)]
    bp = x_prompt.shape[0]
    dt = x_prompt.dtype
    zeros = lambda *s: jnp.zeros(s, dt)
    shared = (_prep_in_proj(w_in_even, w_in_odd), even_p, odd_p, mlp_p, norm_mix, norm_mlp, norm_final)
    y_p, rwkv_p, shift_p, gdn_p, gconv_p, lru_p, lconv_p = _trunk(
        x_prompt,
        zeros(n_even, bp, RWKV_HEADS, RWKV_HD, RWKV_HD), zeros(n_even, bp, RWKV_PROJ),
        zeros(n_even, bp, GDN_HEADS, GDN_HD, GDN_HD), zeros(n_even, bp, CONV_W - 1, GDN_QKV),
        zeros(n_odd, bp, LRU_W), zeros(n_odd, bp, CONV_W - 1, LRU_W), *shared)
    y_s, rwkv_s, shift_s, gdn_s, gconv_s, lru_s, lconv_s = _trunk(
        x_sample, state_rwkv, state_rwkv_shift, state_gdn, state_gdn_conv, state_lru, state_lru_conv, *shared)
    return (y_p, y_s, rwkv_p, rwkv_s, shift_p, shift_s, gdn_p, gdn_s, gconv_p, gconv_s, lru_p, lru_s,
            lconv_p, lconv_s)
```

```python
import functools

import jax
import jax.numpy as jnp
from jax import lax
from jax.experimental import pallas as pl
from jax.experimental.pallas import tpu as pltpu

F32 = jnp.float32
BF16 = jnp.bfloat16

D_MODEL = 1024
DEPTH = 4
CONV_W = 4
NORM_EPS = 1e-6
RWKV_HEADS = 8
RWKV_HD = 64
RWKV_W = RWKV_HEADS * RWKV_HD
RWKV_PAIRS = RWKV_HEADS // 2
DECAY_LORA = 64
AAA_LORA = 64
GATE_LORA = 128
RWKV_PROJ = 3 * RWKV_W + DECAY_LORA + AAA_LORA + GATE_LORA
RWKV_GN_EPS = 64e-5
RWKV_DECAY_SCALE = 0.6065306597126334
GDN_HEADS = 4
GDN_HD = 128
GDN_W = GDN_HEADS * GDN_HD
GDN_QKV = 3 * GDN_W
GDN_BA_PAD = 128
LRU_W = D_MODEL
LRU_HEADS = 8
LRU_BW = LRU_W // LRU_HEADS
LRU_C = 8.0
D_FF = 4 * D_MODEL

SUBLANES = 8
LANES = 128
VMEM_LIMIT_BYTES = 56 * 1024 * 1024


def _mm(a, b):
    return jnp.dot(a.astype(BF16), b.astype(BF16), preferred_element_type=F32)


def _mm_nt(a, b):
    return lax.dot_general(a.astype(BF16), b.astype(BF16), (((1,), (1,)), ((), ())),
                           preferred_element_type=F32)


def _mm_tn(a, b):
    return lax.dot_general(a.astype(BF16), b.astype(BF16), (((0,), (0,)), ((), ())),
                           preferred_element_type=F32)


def _split2(x):
    hi = x.astype(BF16)
    return hi, (x - hi.astype(F32)).astype(BF16)


def _mm_exact_lhs(m01, x):
    hi, lo = _split2(x)
    d = lambda t: jnp.dot(m01, t, preferred_element_type=F32)
    return d(hi) + d(lo)


def _mm_exact_rhs(x, m01):
    hi, lo = _split2(x)
    d = lambda t: jnp.dot(t, m01, preferred_element_type=F32)
    return d(hi) + d(lo)


def _mm3(a, b):
    ah, al = _split2(a)
    bh, bl = _split2(b)
    d = lambda x, y: jnp.dot(x, y, preferred_element_type=F32)
    return d(ah, bh) + d(ah, bl) + d(al, bh)


def _softplus(x):
    return jnp.maximum(x, 0.0) + jnp.log1p(jnp.exp(-jnp.abs(x)))


def _sigmoid(x):
    return jax.nn.sigmoid(x)


def _silu(x):
    return x * jax.nn.sigmoid(x)


def _gelu_tanh(x):
    return 0.5 * x * (1.0 + jnp.tanh(0.7978845608028654 * (x + 0.044715 * (x * x * x))))


def _block_masks(n, c):
    row = lax.broadcasted_iota(jnp.int32, (n, n), 0)
    col = lax.broadcasted_iota(jnp.int32, (n, n), 1)
    same = jnp.bitwise_xor(row, col) < c
    return same & (row >= col), same & (row > col), same, row == col


def _unit_lower_inverses(ms, eye, c):
    n = eye.shape[0]
    row = lax.broadcasted_iota(jnp.int32, (n, n), 0)
    col = lax.broadcasted_iota(jnp.int32, (n, n), 1)
    corner = lambda s: ((jnp.bitwise_xor(row, col) < 2 * s) & (jnp.bitwise_and(row, s) != 0)
                        & (jnp.bitwise_and(col, s) == 0))
    ps = [eye + jnp.where(corner(1), m, 0.0) for m in ms]
    s = 2
    while s < c:
        mask = corner(s)
        pb = [_mm(p, jnp.where(mask, m, 0.0)) for p, m in zip(ps, ms)]
        ps = [p + _mm(x, p) for p, x in zip(ps, pb)]
        s *= 2
    return ps


def _rows(x, s, c):
    return x[s * c:(s + 1) * c]


def _cat(xs, axis=0):
    return xs[0] if len(xs) == 1 else jnp.concatenate(xs, axis=axis)


def _const_spec(shape):
    nd = len(shape)
    return pl.BlockSpec(shape, lambda *_: (0,) * nd)


def _causal_conv(x_ref, xp_ref, cw_ref, s, c):
    base = SUBLANES
    nb = CONV_W - 1
    xp_ref[s, base:base + c, :] = x_ref[s]
    y = cw_ref[0:1, :] * xp_ref[s, base - nb:base - nb + c, :]
    for j in range(1, CONV_W):
        y = y + cw_ref[j:j + 1, :] * xp_ref[s, base - nb + j:base - nb + j + c, :]
    xp_ref[s, base - nb:base, :] = xp_ref[s, base + c - nb:base + c, :]
    return y


def _norm_proj_kernel(x_ref, nw_ref, *refs, groups, col_chunk):
    n_w = len(groups)
    w_refs, o_refs = refs[:n_w], refs[n_w:]
    x = x_ref[...]
    var = jnp.mean(x * x, axis=-1, keepdims=True)
    xn = (x * lax.rsqrt(var + NORM_EPS) * nw_ref[...]).astype(BF16)
    o_iter = iter(o_refs)
    for w_ref, widths in zip(w_refs, groups):
        off = 0
        for width in widths:
            o_ref = next(o_iter)
            for c0 in range(0, width, col_chunk):
                cw = min(col_chunk, width - c0)
                o_ref[:, c0:c0 + cw] = jnp.dot(xn, w_ref[:, off + c0:off + c0 + cw],
                                               preferred_element_type=F32)
            off += width


def _norm_proj(x, nw, ws, layer, groups, tm):
    n = x.shape[0]
    assert n % tm == 0 and all(sum(gr) == w.shape[2] for gr, w in zip(groups, ws))
    kern = functools.partial(_norm_proj_kernel, groups=tuple(tuple(gr) for gr in groups), col_chunk=512)
    widths = [wd for gr in groups for wd in gr]
    return pl.pallas_call(
        kern,
        grid=(n // tm,),
        in_specs=[
            pl.BlockSpec((tm, D_MODEL), lambda i: (i, 0)),
            _const_spec((1, D_MODEL)),
        ] + [pl.BlockSpec((None, D_MODEL, w.shape[2]), lambda i: (layer, 0, 0)) for w in ws],
        out_specs=[pl.BlockSpec((tm, wd), lambda i: (i, 0)) for wd in widths],
        out_shape=[jax.ShapeDtypeStruct((n, wd), F32) for wd in widths],
        compiler_params=pltpu.CompilerParams(
            dimension_semantics=("parallel",), vmem_limit_bytes=VMEM_LIMIT_BYTES),
        name="norm_proj",
    )(x, nw.reshape(1, D_MODEL), *ws)


def _mlp_kernel(*refs, n_mix, ff_chunk, final):
    h_ref = refs[0]
    y_refs = refs[1:1 + n_mix]
    wo_ref, nw_ref, up_ref, down_ref = refs[1 + n_mix:5 + n_mix]
    rest = refs[5 + n_mix:]
    if final:
        fnw_ref, o_ref, yf_ref = rest
    else:
        (o_ref,) = rest
    y = _cat([y_ref[...].astype(BF16) for y_ref in y_refs], axis=1)
    h = h_ref[...] + jnp.dot(y, wo_ref[...], preferred_element_type=F32)
    var = jnp.mean(h * h, axis=-1, keepdims=True)
    xn = (h * lax.rsqrt(var + NORM_EPS) * nw_ref[...]).astype(BF16)
    acc = h
    for c0 in range(0, D_FF, ff_chunk):
        u = jnp.dot(xn, up_ref[:, c0:c0 + ff_chunk], preferred_element_type=F32)
        r = jnp.square(jnp.maximum(u, 0.0)).astype(BF16)
        acc = acc + jnp.dot(r, down_ref[c0:c0 + ff_chunk, :], preferred_element_type=F32)
    o_ref[...] = acc
    if final:
        var2 = jnp.mean(acc * acc, axis=-1, keepdims=True)
        yf_ref[...] = acc * lax.rsqrt(var2 + NORM_EPS) * fnw_ref[...]


def _mix_out_mlp(h, ys, wo, nw, up, down, final_nw, tm):
    n = h.shape[0]
    n_mix = len(ys)
    final = final_nw is not None
    kern = functools.partial(_mlp_kernel, n_mix=n_mix, ff_chunk=512, final=final)
    row = lambda wd: pl.BlockSpec((tm, wd), lambda i: (i, 0))
    in_specs = [row(D_MODEL)] + [row(y.shape[1]) for y in ys]
    in_specs += [_const_spec(wo.shape), _const_spec((1, D_MODEL)), _const_spec((D_MODEL, D_FF)),
                 _const_spec((D_FF, D_MODEL))]
    args = [h, *ys, wo, nw.reshape(1, D_MODEL), up, down]
    out_specs = [row(D_MODEL)]
    out_shape = [jax.ShapeDtypeStruct((n, D_MODEL), F32)]
    if final:
        in_specs.append(_const_spec((1, D_MODEL)))
        args.append(final_nw.reshape(1, D_MODEL))
        out_specs.append(row(D_MODEL))
        out_shape.append(jax.ShapeDtypeStruct((n, D_MODEL), F32))
    return pl.pallas_call(
        kern,
        grid=(n // tm,),
        in_specs=in_specs,
        out_specs=out_specs,
        out_shape=out_shape,
        compiler_params=pltpu.CompilerParams(
            dimension_semantics=("parallel",), vmem_limit_bytes=VMEM_LIMIT_BYTES),
        name="mix_out_mlp",
    )(*args)


def _rwkv_kernel(p_ref, prev_ref, s0_ref, mu_ref, w0_ref, w2_ref, a0_ref, a2_ref, g2_ref,
                 kk_ref, ka_ref, rk_ref, lnw_ref, lnb_ref, hsum_ref,
                 out_ref, s_out_ref, xp_ref, state_ref, *, c, g, nc):
    ci = pl.program_id(1)
    base = SUBLANES
    gc = g * c

    hd = RWKV_HD

    @pl.when(ci == 0)
    def _():
        for s in range(g):
            xp_ref[s, base - 1:base, :] = prev_ref[s]
        state_ref[...] = jnp.zeros(state_ref.shape, F32)
        for s in range(g):
            for j in range(RWKV_PAIRS):
                state_ref[s, j, 0:hd, 0:hd] = s0_ref[s, 2 * j]
                state_ref[s, j, hd:2 * hd, hd:2 * hd] = s0_ref[s, 2 * j + 1]

    shifted = []
    for s in range(g):
        ps = p_ref[s]
        xp_ref[s, base:base + c, :] = ps
        shifted.append(xp_ref[s, base - 1:base - 1 + c, :])
        xp_ref[s, base - 1:base, :] = ps[c - 1:c, :]
    p = p_ref[...].reshape(gc, RWKV_PROJ)
    shifted = _cat(shifted)

    xs = p + mu_ref[...] * (shifted - p)
    r = xs[:, 0:RWKV_W]
    k = xs[:, RWKV_W:2 * RWKV_W]
    v = xs[:, 2 * RWKV_W:3 * RWKV_W]
    xwa = xs[:, 3 * RWKV_W:3 * RWKV_W + DECAY_LORA + AAA_LORA]
    xg = xs[:, 3 * RWKV_W + DECAY_LORA + AAA_LORA:]

    ld = -RWKV_DECAY_SCALE * _sigmoid(w0_ref[...] + _mm(jnp.tanh(xwa), w2_ref[...]))
    a = _sigmoid(a0_ref[...] + _mm(xwa, a2_ref[...]))
    gate = _mm(_sigmoid(xg), g2_ref[...])

    incl1, _, same1, _ = _block_masks(gc, c)
    sums = _mm_exact_lhs(jnp.concatenate([incl1, same1], axis=0).astype(BF16), ld)
    cum = sums[:gc]
    tot = sums[gc:]
    w_in = jnp.exp(cum)
    w_ex = jnp.exp(cum - ld)
    w_inv = jnp.exp(-cum)
    w_tail = jnp.exp(tot - cum)
    w_last = jnp.exp(tot)

    hsum = hsum_ref[...]

    def head_sums(x):
        return jnp.concatenate(
            [_mm_exact_rhs(x[:, j * LANES:(j + 1) * LANES], hsum) for j in range(RWKV_PAIRS)], axis=1)

    kkr = k * kk_ref[...]
    kk = kkr * lax.rsqrt(head_sums(kkr * kkr) + 1e-6)
    kp = k * (1.0 + (a - 1.0) * ka_ref[...])
    bv = kk * a

    lo = lax.broadcasted_iota(jnp.int32, (gc, LANES), 1) < RWKV_HD

    def heads(x):
        out = []
        for j in range(RWKV_PAIRS):
            slab = x[:, j * LANES:(j + 1) * LANES]
            out += [jnp.where(lo, slab, 0.0), jnp.where(lo, 0.0, slab)]
        return out

    a_t = heads(-kk * w_ex)
    r_t = heads(r * w_in)
    b_t = heads(bv * w_inv)
    k_t = heads(kp * w_inv)
    v_b = heads(v)
    b_w = heads(bv * w_tail)
    k_w = heads(kp * w_tail)

    hs = range(RWKV_HEADS)
    incl, strict, _, diag = _block_masks(gc, c)
    eye = diag.astype(F32)
    ars = [jnp.concatenate([a_t[h], r_t[h]], axis=0) for h in hs]
    gbs = [_mm_nt(ars[h], b_t[h]) for h in hs]
    gks = [_mm_nt(ars[h], k_t[h]) for h in hs]
    m_ab = [jnp.where(strict, x[:gc], 0.0) for x in gbs]
    m_ak = [jnp.where(strict, x[:gc], 0.0) for x in gks]
    l_rb = [jnp.where(incl, x[gc:], 0.0) for x in gbs]
    l_rk = [jnp.where(incl, x[gc:], 0.0) for x in gks]
    t_inv = _unit_lower_inverses(m_ab, eye, c)

    sa = [[] for _ in hs]
    sr = [[] for _ in hs]
    for j in range(RWKV_PAIRS):
        for s in range(g):
            lhs = jnp.concatenate([_rows(x[2 * j + i], s, c) for x in (a_t, r_t) for i in range(2)], axis=0)
            ps = _mm_nt(lhs, state_ref[s, j])
            for i in range(2):
                sa[2 * j + i].append(_rows(ps, i, c))
                sr[2 * j + i].append(_rows(ps, 2 + i, c))
    sa = [_cat(x) for x in sa]
    sr = [_cat(x) for x in sr]

    mv = [_mm(m_ak[h], v_b[h]) for h in hs]
    us = [_mm3(t_inv[h], sa[h] + mv[h]) for h in hs]
    ys = [sr[h] + _mm(l_rb[h], us[h]) + _mm(l_rk[h], v_b[h]) for h in hs]
    y = jnp.concatenate([ys[2 * j] + ys[2 * j + 1] for j in range(RWKV_PAIRS)], axis=1)

    for j in range(RWKV_PAIRS):
        for s in range(g):
            uv = jnp.concatenate([_rows(x[2 * j + i], s, c) for x in (us, v_b) for i in range(2)], axis=0)
            bk = jnp.concatenate([_rows(x[2 * j + i], s, c) for x in (b_w, k_w) for i in range(2)], axis=0)
            decay = w_last[s * c:s * c + 1, j * LANES:(j + 1) * LANES]
            state_ref[s, j] = state_ref[s, j] * decay + _mm_tn(uv, bk)

    inv_hd = 1.0 / RWKV_HD
    sums = head_sums(jnp.concatenate([y, y * y, r * kp * rk_ref[...]], axis=0))
    mean = sums[:gc] * inv_hd
    var = sums[gc:2 * gc] * inv_hd - mean * mean
    yn = (y - mean) * lax.rsqrt(var + RWKV_GN_EPS) * lnw_ref[...] + lnb_ref[...]
    out_ref[...] = ((yn + sums[2 * gc:] * v) * gate).reshape(g, c, RWKV_W)

    @pl.when(ci == nc - 1)
    def _():
        for s in range(g):
            for j in range(RWKV_PAIRS):
                s_out_ref[s, 2 * j] = state_ref[s, j, 0:hd, 0:hd]
                s_out_ref[s, 2 * j + 1] = state_ref[s, j, hd:2 * hd, hd:2 * hd]


def _rwkv_mix(p, prev, s0, layer, prm, c, g):
    b, t, _ = p.shape
    nc = t // c
    assert nc * c == t and c % SUBLANES == 0 and b % g == 0
    kern = functools.partial(_rwkv_kernel, c=c, g=g, nc=nc)
    consts = [prm["mu"], prm["w0"], prm["w2"], prm["a0"], prm["a2"], prm["g2"], prm["k_k"], prm["k_a"],
              prm["r_k"], prm["ln_w"], prm["ln_b"], prm["hsum"]]
    st_block = (g, RWKV_HEADS, RWKV_HD, RWKV_HD)
    return pl.pallas_call(
        kern,
        grid=(b // g, nc),
        in_specs=[
            pl.BlockSpec((g, c, RWKV_PROJ), lambda i, j: (i, j, 0)),
            pl.BlockSpec((g, 1, RWKV_PROJ), lambda i, j: (i, 0, 0)),
            pl.BlockSpec((None,) + st_block, lambda i, j: (layer, i, 0, 0, 0)),
        ] + [_const_spec(x.shape) for x in consts],
        out_specs=[
            pl.BlockSpec((g, c, RWKV_W), lambda i, j: (i, j, 0)),
            pl.BlockSpec(st_block, lambda i, j: (i, 0, 0, 0)),
        ],
        out_shape=[
            jax.ShapeDtypeStruct((b, t, RWKV_W), F32),
            jax.ShapeDtypeStruct((b,) + st_block[1:], F32),
        ],
        scratch_shapes=[
            pltpu.VMEM((g, c + SUBLANES, RWKV_PROJ), F32),
            pltpu.VMEM((g, RWKV_PAIRS, LANES, LANES), F32),
        ],
        compiler_params=pltpu.CompilerParams(
            dimension_semantics=("parallel", "arbitrary"), vmem_limit_bytes=VMEM_LIMIT_BYTES),
        name="rwkv7_mix",
    )(p, prev.reshape(b, 1, RWKV_PROJ), s0, *consts)


def _gdn_kernel(qkv_ref, z_ref, ba_ref, buf_ref, s0_ref, cw_ref, alog_ref, dt_ref, nw_ref,
                out_ref, s_out_ref, xp_ref, state_ref, *, c, g, nc):
    ci = pl.program_id(1)
    base = SUBLANES
    nb = CONV_W - 1
    gc = g * c

    @pl.when(ci == 0)
    def _():
        for s in range(g):
            xp_ref[s, base - nb:base, :] = buf_ref[s]
        state_ref[...] = s0_ref[...]

    qkv = _silu(_cat([_causal_conv(qkv_ref, xp_ref, cw_ref, s, c) for s in range(g)]))
    z = z_ref[...].reshape(gc, GDN_W)
    ba = ba_ref[...].reshape(gc, GDN_BA_PAD)
    beta_all = _sigmoid(ba)
    g_all = -jnp.exp(alog_ref[...]) * _softplus(ba + dt_ref[...])
    incl, strict, same, diag = _block_masks(gc, c)
    eye = diag.astype(F32)
    sums = _mm_exact_lhs(jnp.concatenate([incl, same], axis=0).astype(BF16), g_all)
    gc_all = sums[:gc]
    tot_all = sums[gc:]
    upper = (same & jnp.logical_not(strict)).astype(BF16)
    dn = (((0,), (0,)), ((), ()))
    gct = sum(lax.dot_general(t, upper, dn, preferred_element_type=F32) for t in _split2(g_all))

    hs = range(GDN_HEADS)
    head = lambda x, part, h: x[:, part * GDN_W + h * GDN_HD:part * GDN_W + (h + 1) * GDN_HD]
    q = [head(qkv, 0, h) for h in hs]
    k = [head(qkv, 1, h) for h in hs]
    v = [head(qkv, 2, h) for h in hs]
    q = [x * lax.rsqrt(jnp.sum(x * x, axis=-1, keepdims=True) + 1e-6) * (GDN_HD ** -0.5) for x in q]
    k = [x * lax.rsqrt(jnp.sum(x * x, axis=-1, keepdims=True) + 1e-6) for x in k]
    beta = [beta_all[:, h:h + 1] for h in hs]
    gcol = [gc_all[:, GDN_HEADS + h:GDN_HEADS + h + 1] for h in hs]
    grow = [gct[GDN_HEADS + h:GDN_HEADS + h + 1, :] for h in hs]
    glast = [tot_all[:, GDN_HEADS + h:GDN_HEADS + h + 1] for h in hs]
    decay = [jnp.where(incl, jnp.exp(jnp.where(incl, gcol[h] - grow[h], 0.0)), 0.0) for h in hs]
    kb = [k[h] * beta[h] for h in hs]
    vb = [v[h] * beta[h] for h in hs]
    eg = [jnp.exp(gcol[h]) for h in hs]
    qe = [q[h] * eg[h] for h in hs]
    k_tail = [k[h] * jnp.exp(glast[h] - gcol[h]) for h in hs]
    gq = [_mm_nt(jnp.concatenate([q[h], kb[h]], axis=0), k[h]) for h in hs]
    qk = [jnp.where(incl, gq[h][:gc] * decay[h], 0.0) for h in hs]
    lmat = [jnp.where(strict, gq[h][gc:] * decay[h], 0.0) for h in hs]
    t_inv = _unit_lower_inverses([-x for x in lmat], eye, c)
    uw = [_mm3(t_inv[h], jnp.concatenate([vb[h], kb[h] * eg[h]], axis=1)) for h in hs]

    q_s, w_s = [], []
    for h in hs:
        pieces = []
        for s in range(g):
            lhs = jnp.concatenate([_rows(qe[h], s, c), _rows(uw[h][:, GDN_HD:], s, c)], axis=0)
            pieces.append(_mm(lhs, state_ref[s, h]))
        q_s.append(_cat([x[:c] for x in pieces]))
        w_s.append(_cat([x[c:] for x in pieces]))
    v_new = [uw[h][:, :GDN_HD] - w_s[h] for h in hs]
    o = [q_s[h] + _mm(qk[h], v_new[h]) for h in hs]
    for h in hs:
        for s in range(g):
            decay_s = jnp.exp(glast[h][s * c:s * c + 1, :])
            state_ref[s, h] = state_ref[s, h] * decay_s + _mm_tn(_rows(k_tail[h], s, c), _rows(v_new[h], s, c))
    outs = []
    for h in hs:
        on = o[h] * lax.rsqrt(jnp.mean(o[h] * o[h], axis=-1, keepdims=True) + NORM_EPS) * nw_ref[...]
        outs.append(on * _silu(z[:, h * GDN_HD:(h + 1) * GDN_HD]))
    out_ref[...] = jnp.concatenate(outs, axis=1).reshape(g, c, GDN_W)

    @pl.when(ci == nc - 1)
    def _():
        s_out_ref[...] = state_ref[...]


def _gdn_mix(qkv, z, ba, buf, s0, layer, prm, c, g):
    b, t, _ = qkv.shape
    nc = t // c
    assert nc * c == t and c % SUBLANES == 0 and b % g == 0
    kern = functools.partial(_gdn_kernel, c=c, g=g, nc=nc)
    consts = [prm["conv_w"], prm["a_log"], prm["dt_bias"], prm["norm_w"]]
    st_block = (g, GDN_HEADS, GDN_HD, GDN_HD)
    return pl.pallas_call(
        kern,
        grid=(b // g, nc),
        in_specs=[
            pl.BlockSpec((g, c, GDN_QKV), lambda i, j: (i, j, 0)),
            pl.BlockSpec((g, c, GDN_W), lambda i, j: (i, j, 0)),
            pl.BlockSpec((g, c, GDN_BA_PAD), lambda i, j: (i, j, 0)),
            pl.BlockSpec((g, CONV_W - 1, GDN_QKV), lambda i, j: (i, 0, 0)),
            pl.BlockSpec((None,) + st_block, lambda i, j: (layer, i, 0, 0, 0)),
        ] + [_const_spec(x.shape) for x in consts],
        out_specs=[
            pl.BlockSpec((g, c, GDN_W), lambda i, j: (i, j, 0)),
            pl.BlockSpec(st_block, lambda i, j: (i, 0, 0, 0)),
        ],
        out_shape=[
            jax.ShapeDtypeStruct((b, t, GDN_W), F32),
            jax.ShapeDtypeStruct((b,) + st_block[1:], F32),
        ],
        scratch_shapes=[
            pltpu.VMEM((g, c + SUBLANES, GDN_QKV), F32),
            pltpu.VMEM(st_block, F32),
        ],
        compiler_params=pltpu.CompilerParams(
            dimension_semantics=("parallel", "arbitrary"), vmem_limit_bytes=VMEM_LIMIT_BYTES),
        name="gdn_mix",
    )(qkv, z, ba, buf, s0, *consts)


def _lru_kernel(gate_ref, xb_ref, buf_ref, h0_ref, cw_ref, cb_ref, wab_ref, bab_ref, l_ref,
                out_ref, h_out_ref, xp_ref, carry_ref, sa_ref, sb_ref, *, c, g, nc):
    ci = pl.program_id(1)
    base = SUBLANES
    nb = CONV_W - 1
    gc = g * c

    @pl.when(ci == 0)
    def _():
        for s in range(g):
            xp_ref[s, base - nb:base, :] = buf_ref[s]
        carry_ref[...] = h0_ref[...]

    xc = _cat([_causal_conv(xb_ref, xp_ref, cw_ref, s, c) for s in range(g)]) + cb_ref[...]

    for h in range(LRU_HEADS):
        sl = slice(h * LRU_BW, (h + 1) * LRU_BW)
        xh = xc[:, sl]
        ri = _mm(xh, wab_ref[h]) + bab_ref[h]
        r = _sigmoid(ri[:, :LRU_BW])
        i = _sigmoid(ri[:, LRU_BW:])
        log_a = -LRU_C * r * _softplus(-l_ref[:, sl])
        a = jnp.exp(log_a)
        mult = jnp.sqrt(-jnp.tanh(log_a) * (a * a + 1.0))
        sa_ref[:, sl] = a
        sb_ref[:, sl] = mult * i * xh

    tpos = jnp.bitwise_and(lax.broadcasted_iota(jnp.int32, (gc, 1), 0), SUBLANES - 1)
    a = sa_ref[...]
    b = sb_ref[...]

    def shift_in_group(x, d):
        x3 = x.reshape(gc // SUBLANES, SUBLANES, LRU_W)
        return pltpu.roll(x3, d, axis=1).reshape(gc, LRU_W)

    d = 1
    while d < SUBLANES:
        live = tpos >= d
        a_sh = jnp.where(live, shift_in_group(a, d), 1.0)
        b_sh = jnp.where(live, shift_in_group(b, d), 0.0)
        b = a * b_sh + b
        a = a * a_sh
        d *= 2
    groups = []
    for s in range(g):
        h = carry_ref[s]
        for r0 in range(s * c, (s + 1) * c, SUBLANES):
            hg = a[r0:r0 + SUBLANES] * h + b[r0:r0 + SUBLANES]
            groups.append(hg)
            h = hg[SUBLANES - 1:SUBLANES]
        carry_ref[s] = h
    hs = _cat(groups)
    out_ref[...] = (hs * _gelu_tanh(gate_ref[...].reshape(gc, LRU_W))).reshape(g, c, LRU_W)

    @pl.when(ci == nc - 1)
    def _():
        h_out_ref[...] = carry_ref[...]


def _lru_mix(gate, xb, buf, h0, prm, c, g):
    b, t, _ = xb.shape
    nc = t // c
    assert nc * c == t and c % SUBLANES == 0 and b % g == 0
    kern = functools.partial(_lru_kernel, c=c, g=g, nc=nc)
    consts = [prm["conv_w"], prm["conv_b"], prm["wab"], prm["bab"], prm["l"]]
    seq = lambda rows: pl.BlockSpec((g, rows, LRU_W), lambda i, j: (i, 0, 0))
    chunk = pl.BlockSpec((g, c, LRU_W), lambda i, j: (i, j, 0))
    y, h_last = pl.pallas_call(
        kern,
        grid=(b // g, nc),
        in_specs=[chunk, chunk, seq(CONV_W - 1), seq(1)] + [_const_spec(x.shape) for x in consts],
        out_specs=[chunk, seq(1)],
        out_shape=[
            jax.ShapeDtypeStruct((b, t, LRU_W), F32),
            jax.ShapeDtypeStruct((b, 1, LRU_W), F32),
        ],
        scratch_shapes=[
            pltpu.VMEM((g, c + SUBLANES, LRU_W), F32),
            pltpu.VMEM((g, 1, LRU_W), F32),
            pltpu.VMEM((g * c, LRU_W), F32),
            pltpu.VMEM((g * c, LRU_W), F32),
        ],
        compiler_params=pltpu.CompilerParams(
            dimension_semantics=("parallel", "arbitrary"), vmem_limit_bytes=VMEM_LIMIT_BYTES),
        name="lru_mix",
    )(gate, xb, buf, h0.reshape(b, 1, LRU_W), *consts)
    return y, h_last.reshape(b, LRU_W)


def _prep_even(i, w_out_even, rwkv_mu, rwkv_w0, rwkv_w2, rwkv_a0, rwkv_a2, rwkv_g2, rwkv_k_k,
               rwkv_k_a, rwkv_r_k, rwkv_ln_w, rwkv_ln_b, gdn_conv_w, gdn_A_log, gdn_dt_bias, gdn_norm_w):
    n_ba = 2 * GDN_HEADS
    row = lambda x: x.reshape(1, -1)
    lane = jnp.arange(LANES) // RWKV_HD
    lora = DECAY_LORA + AAA_LORA
    rwkv = dict(
        mu=row(rwkv_mu[i]), w0=row(rwkv_w0[i]), a0=row(rwkv_a0[i]),
        w2=jnp.pad(rwkv_w2[i], ((0, lora - DECAY_LORA), (0, 0))).astype(BF16),
        a2=jnp.pad(rwkv_a2[i], ((DECAY_LORA, 0), (0, 0))).astype(BF16),
        g2=rwkv_g2[i].astype(BF16),
        k_k=row(rwkv_k_k[i]), k_a=row(rwkv_k_a[i]), r_k=row(rwkv_r_k[i]),
        ln_w=row(rwkv_ln_w[i]), ln_b=row(rwkv_ln_b[i]),
        hsum=(lane[:, None] == lane[None, :]).astype(BF16),
    )
    pad_heads = lambda x: jnp.pad(x, (GDN_HEADS, GDN_BA_PAD - n_ba)).reshape(1, GDN_BA_PAD)
    gdn = dict(conv_w=gdn_conv_w[i], a_log=pad_heads(gdn_A_log[i]), dt_bias=pad_heads(gdn_dt_bias[i]),
               norm_w=row(gdn_norm_w[i]))
    return rwkv, gdn, w_out_even[i].astype(BF16)


def _prep_odd(i, w_out_odd, lru_conv_w, lru_conv_b, lru_wa, lru_ba, lru_wi, lru_bi, lru_L):
    lru = dict(
        conv_w=lru_conv_w[i], conv_b=lru_conv_b[i].reshape(1, LRU_W),
        wab=jnp.concatenate([lru_wa[i], lru_wi[i]], axis=-1).astype(BF16),
        bab=jnp.concatenate([lru_ba[i], lru_bi[i]], axis=-1).reshape(LRU_HEADS, 1, 2 * LRU_BW),
        l=lru_L[i].reshape(1, LRU_W),
    )
    return lru, w_out_odd[i].astype(BF16)


def _prep_in_proj(w_in_even, w_in_odd):
    main = RWKV_PROJ + GDN_QKV + GDN_W
    n_ba = 2 * GDN_HEADS
    even_ba = jnp.pad(w_in_even[:, :, main:], ((0, 0), (0, 0), (0, GDN_BA_PAD - n_ba)))
    return w_in_even[:, :, :main].astype(BF16), even_ba.astype(BF16), w_in_odd.astype(BF16)


def _row_tile(n):
    return 512 if n % 512 == 0 else 256


def _seq_group(b, c, rows):
    g = max(1, min(b, rows // c))
    while b % g:
        g -= 1
    return g


def _trunk(x, st_rwkv, st_shift, st_gdn, st_gconv, st_lru, st_lconv, in_w, even_p, odd_p, mlp_p, norm_mix,
           norm_mlp, norm_final):
    b, t, _ = x.shape
    n = b * t
    tm = _row_tile(n)
    c_even = min(64, t)
    g_rwkv = _seq_group(b, c_even, 128)
    g_gdn = _seq_group(b, c_even, 128)
    c_odd = min(256, t)
    g_odd = _seq_group(b, c_odd, 256)
    h = x.reshape(n, D_MODEL)
    n_rwkv, n_shift, n_gdn, n_gconv, n_lru, n_lconv = [], [], [], [], [], []
    y_final = None
    for l in range(DEPTH):
        i = l // 2
        up, down = mlp_p[l]
        final_nw = norm_final if l == DEPTH - 1 else None
        if l % 2 == 0:
            rwkv, gdn, wo = even_p[i]
            p, qkv, z, ba = _norm_proj(h, norm_mix[l], in_w[:2], i,
                                       ((RWKV_PROJ, GDN_QKV, GDN_W), (GDN_BA_PAD,)), tm)
            p = p.reshape(b, t, RWKV_PROJ)
            qkv = qkv.reshape(b, t, GDN_QKV)
            ya, s_a = _rwkv_mix(p, st_shift[i], st_rwkv, i, rwkv, c_even, g_rwkv)
            yb, s_b = _gdn_mix(qkv, z.reshape(b, t, GDN_W), ba.reshape(b, t, GDN_BA_PAD), st_gconv[i],
                               st_gdn, i, gdn, c_even, g_gdn)
            n_rwkv.append(s_a)
            n_shift.append(p[:, -1])
            n_gdn.append(s_b)
            n_gconv.append(qkv[:, t - (CONV_W - 1):])
            ys = (ya.reshape(n, RWKV_W), yb.reshape(n, GDN_W))
        else:
            lru, wo = odd_p[i]
            gate, xb = _norm_proj(h, norm_mix[l], in_w[2:], i, ((LRU_W, LRU_W),), tm)
            xb = xb.reshape(b, t, LRU_W)
            yc, h_last = _lru_mix(gate.reshape(b, t, LRU_W), xb, st_lconv[i], st_lru[i], lru, c_odd, g_odd)
            n_lru.append(h_last)
            n_lconv.append(xb[:, t - (CONV_W - 1):])
            ys = (yc.reshape(n, LRU_W),)
        outs = _mix_out_mlp(h, ys, wo, norm_mlp[l], up, down, final_nw, tm)
        h = outs[0]
        if final_nw is not None:
            y_final = outs[1]
    return (y_final.reshape(b, t, D_MODEL), jnp.stack(n_rwkv), jnp.stack(n_shift), jnp.stack(n_gdn),
            jnp.stack(n_gconv), jnp.stack(n_lru), jnp.stack(n_lconv))


def kernel(x_prompt, x_sample, state_rwkv, state_rwkv_shift, state_gdn, state_gdn_conv, state_lru, state_lru_conv, norm_mix, norm_mlp, norm_final, w_in_even, w_out_even, rwkv_mu, rwkv_w0, rwkv_w2, rwkv_a0, rwkv_a2, rwkv_g2, rwkv_k_k, rwkv_k_a, rwkv_r_k, rwkv_ln_w, rwkv_ln_b, gdn_conv_w, gdn_A_log, gdn_dt_bias, gdn_norm_w, w_in_odd, w_out_odd, lru_conv_w, lru_conv_b, lru_wa, lru_ba, lru_wi, lru_bi, lru_L, mlp_up, mlp_down):
    n_even = (DEPTH + 1) // 2
    n_odd = DEPTH // 2
    even_p = [_prep_even(i, w_out_even, rwkv_mu, rwkv_w0, rwkv_w2, rwkv_a0, rwkv_a2, rwkv_g2,
                         rwkv_k_k, rwkv_k_a, rwkv_r_k, rwkv_ln_w, rwkv_ln_b, gdn_conv_w, gdn_A_log,
                         gdn_dt_bias, gdn_norm_w) for i in range(n_even)]
    odd_p = [_prep_odd(i, w_out_odd, lru_conv_w, lru_conv_b, lru_wa, lru_ba, lru_wi, lru_bi, lru_L)
             for i in range(n_odd)]
    mlp_p = [(mlp_up[l].astype(BF16), mlp_down[l].astype(BF16)) for l in range(DEPTH)]
    bp = x_prompt.shape[0]
    dt = x_prompt.dtype
    zeros = lambda *s: jnp.zeros(s, dt)
    shared = (_prep_in_proj(w_in_even, w_in_odd), even_p, odd_p, mlp_p, norm_mix, norm_mlp, norm_final)
    y_p, rwkv_p, shift_p, gdn_p, gconv_p, lru_p, lconv_p = _trunk(
        x_prompt,
        zeros(n_even, bp, RWKV_HEADS, RWKV_HD, RWKV_HD), zeros(n_even, bp, RWKV_PROJ),
        zeros(n_even, bp, GDN_HEADS, GDN_HD, GDN_HD), zeros(n_even, bp, CONV_W - 1, GDN_QKV),
        zeros(n_odd, bp, LRU_W), zeros(n_odd, bp, CONV_W - 1, LRU_W), *shared)
    y_s, rwkv_s, shift_s, gdn_s, gconv_s, lru_s, lconv_s = _trunk(
        x_sample, state_rwkv, state_rwkv_shift, state_gdn, state_gdn_conv, state_lru, state_lru_conv, *shared)
    return (y_p, y_s, rwkv_p, rwkv_s, shift_p, shift_s, gdn_p, gdn_s, gconv_p, gconv_s, lru_p, lru_s,
            lconv_p, lconv_s)
```

```python
import functools

import jax
import jax.numpy as jnp
from jax import lax
from jax.experimental import pallas as pl
from jax.experimental.pallas import tpu as pltpu

F32 = jnp.float32
BF16 = jnp.bfloat16

D_MODEL = 1024
DEPTH = 4
CONV_W = 4
NORM_EPS = 1e-6
RWKV_HEADS = 8
RWKV_HD = 64
RWKV_W = RWKV_HEADS * RWKV_HD
RWKV_PAIRS = RWKV_HEADS // 2
DECAY_LORA = 64
AAA_LORA = 64
GATE_LORA = 128
RWKV_PROJ = 3 * RWKV_W + DECAY_LORA + AAA_LORA + GATE_LORA
RWKV_GN_EPS = 64e-5
RWKV_DECAY_SCALE = 0.6065306597126334
GDN_HEADS = 4
GDN_HD = 128
GDN_W = GDN_HEADS * GDN_HD
GDN_QKV = 3 * GDN_W
GDN_BA_PAD = 128
LRU_W = D_MODEL
LRU_HEADS = 8
LRU_BW = LRU_W // LRU_HEADS
LRU_C = 8.0
D_FF = 4 * D_MODEL

SUBLANES = 8
LANES = 128
VMEM_LIMIT_BYTES = 56 * 1024 * 1024
MAX_SEQS_PER_STEP = 16


def _mm(a, b):
    return jnp.dot(a.astype(BF16), b.astype(BF16), preferred_element_type=F32)


def _mm_nt(a, b):
    return lax.dot_general(a.astype(BF16), b.astype(BF16), (((1,), (1,)), ((), ())),
                           preferred_element_type=F32)


def _mm_tn(a, b):
    return lax.dot_general(a.astype(BF16), b.astype(BF16), (((0,), (0,)), ((), ())),
                           preferred_element_type=F32)


def _split2(x):
    hi = x.astype(BF16)
    return hi, (x - hi.astype(F32)).astype(BF16)


def _mm_exact_lhs(m01, x):
    hi, lo = _split2(x)
    d = lambda t: jnp.dot(m01, t, preferred_element_type=F32)
    return d(hi) + d(lo)


def _mm_exact_rhs(x, m01):
    hi, lo = _split2(x)
    d = lambda t: jnp.dot(t, m01, preferred_element_type=F32)
    return d(hi) + d(lo)


def _mm3(a, b):
    ah, al = _split2(a)
    bh, bl = _split2(b)
    d = lambda x, y: jnp.dot(x, y, preferred_element_type=F32)
    return d(ah, bh) + d(ah, bl) + d(al, bh)


def _softplus(x):
    return jnp.maximum(x, 0.0) + jnp.log1p(jnp.exp(-jnp.abs(x)))


def _sigmoid(x):
    return jax.nn.sigmoid(x)


def _silu(x):
    return x * jax.nn.sigmoid(x)


def _gelu_tanh(x):
    return 0.5 * x * (1.0 + jnp.tanh(0.7978845608028654 * (x + 0.044715 * (x * x * x))))


def _block_masks(n, c):
    row = lax.broadcasted_iota(jnp.int32, (n, n), 0)
    col = lax.broadcasted_iota(jnp.int32, (n, n), 1)
    same = jnp.bitwise_xor(row, col) < c
    return same & (row >= col), same & (row > col), same, row == col


def _unit_lower_inverses(ms, eye, c):
    n = eye.shape[0]
    row = lax.broadcasted_iota(jnp.int32, (n, n), 0)
    col = lax.broadcasted_iota(jnp.int32, (n, n), 1)
    corner = lambda s: ((jnp.bitwise_xor(row, col) < 2 * s) & (jnp.bitwise_and(row, s) != 0)
                        & (jnp.bitwise_and(col, s) == 0))
    ps = [eye + jnp.where(corner(1), m, 0.0) for m in ms]
    s = 2
    while s < c:
        mask = corner(s)
        pb = [_mm(p, jnp.where(mask, m, 0.0)) for p, m in zip(ps, ms)]
        ps = [p + _mm(x, p) for p, x in zip(ps, pb)]
        s *= 2
    return ps


def _rows(x, s, c):
    return x[s * c:(s + 1) * c]


def _cat(xs, axis=0):
    return xs[0] if len(xs) == 1 else jnp.concatenate(xs, axis=axis)


def _const_spec(shape):
    nd = len(shape)
    return pl.BlockSpec(shape, lambda *_: (0,) * nd)


def _causal_conv(x_ref, xp_ref, cw_ref, s, c):
    base = SUBLANES
    nb = CONV_W - 1
    xp_ref[s, base:base + c, :] = x_ref[s]
    y = cw_ref[0:1, :] * xp_ref[s, base - nb:base - nb + c, :]
    for j in range(1, CONV_W):
        y = y + cw_ref[j:j + 1, :] * xp_ref[s, base - nb + j:base - nb + j + c, :]
    xp_ref[s, base - nb:base, :] = xp_ref[s, base + c - nb:base + c, :]
    return y


def _norm_proj_kernel(x_ref, nw_ref, *refs, groups, col_chunk):
    n_w = len(groups)
    w_refs, o_refs = refs[:n_w], refs[n_w:]
    x = x_ref[...]
    var = jnp.mean(x * x, axis=-1, keepdims=True)
    xn = (x * lax.rsqrt(var + NORM_EPS) * nw_ref[...]).astype(BF16)
    o_iter = iter(o_refs)
    for w_ref, widths in zip(w_refs, groups):
        off = 0
        for width in widths:
            o_ref = next(o_iter)
            for c0 in range(0, width, col_chunk):
                cw = min(col_chunk, width - c0)
                o_ref[:, c0:c0 + cw] = jnp.dot(xn, w_ref[:, off + c0:off + c0 + cw],
                                               preferred_element_type=F32)
            off += width


def _norm_proj(x, nw, ws, layer, groups, tm):
    n = x.shape[0]
    assert n % tm == 0 and all(sum(gr) <= w.shape[2] for gr, w in zip(groups, ws))
    kern = functools.partial(_norm_proj_kernel, groups=tuple(tuple(gr) for gr in groups), col_chunk=512)
    widths = [wd for gr in groups for wd in gr]
    return pl.pallas_call(
        kern,
        grid=(n // tm,),
        in_specs=[
            pl.BlockSpec((tm, D_MODEL), lambda i: (i, 0)),
            _const_spec((1, D_MODEL)),
        ] + [pl.BlockSpec((None, D_MODEL, w.shape[2]), lambda i: (layer, 0, 0)) for w in ws],
        out_specs=[pl.BlockSpec((tm, wd), lambda i: (i, 0)) for wd in widths],
        out_shape=[jax.ShapeDtypeStruct((n, wd), F32) for wd in widths],
        compiler_params=pltpu.CompilerParams(
            dimension_semantics=("parallel",), vmem_limit_bytes=VMEM_LIMIT_BYTES),
        name="norm_proj",
    )(x, nw.reshape(1, D_MODEL), *ws)


def _mlp_kernel(*refs, n_mix, ff_chunk, final):
    h_ref = refs[0]
    y_refs = refs[1:1 + n_mix]
    wo_ref, nw_ref, up_ref, down_ref = refs[1 + n_mix:5 + n_mix]
    rest = refs[5 + n_mix:]
    if final:
        fnw_ref, o_ref, yf_ref = rest
    else:
        (o_ref,) = rest
    y = _cat([y_ref[...].astype(BF16) for y_ref in y_refs], axis=1)
    h = h_ref[...] + jnp.dot(y, wo_ref[...], preferred_element_type=F32)
    var = jnp.mean(h * h, axis=-1, keepdims=True)
    xn = (h * lax.rsqrt(var + NORM_EPS) * nw_ref[...]).astype(BF16)
    acc = h
    for c0 in range(0, D_FF, ff_chunk):
        u = jnp.dot(xn, up_ref[:, c0:c0 + ff_chunk], preferred_element_type=F32)
        r = jnp.square(jnp.maximum(u, 0.0)).astype(BF16)
        acc = acc + jnp.dot(r, down_ref[c0:c0 + ff_chunk, :], preferred_element_type=F32)
    o_ref[...] = acc
    if final:
        var2 = jnp.mean(acc * acc, axis=-1, keepdims=True)
        yf_ref[...] = acc * lax.rsqrt(var2 + NORM_EPS) * fnw_ref[...]


def _mix_out_mlp(h, ys, wo, wo_layer, nw, up, down, layer, final_nw, tm):
    n = h.shape[0]
    n_mix = len(ys)
    final = final_nw is not None
    kern = functools.partial(_mlp_kernel, n_mix=n_mix, ff_chunk=512, final=final)
    row = lambda wd: pl.BlockSpec((tm, wd), lambda i: (i, 0))
    pick = lambda w, l: pl.BlockSpec((None,) + w.shape[1:], lambda i: (l, 0, 0))
    in_specs = [row(D_MODEL)] + [row(y.shape[1]) for y in ys]
    in_specs += [pick(wo, wo_layer), _const_spec((1, D_MODEL)), pick(up, layer), pick(down, layer)]
    args = [h, *ys, wo, nw.reshape(1, D_MODEL), up, down]
    out_specs = [row(D_MODEL)]
    out_shape = [jax.ShapeDtypeStruct((n, D_MODEL), F32)]
    if final:
        in_specs.append(_const_spec((1, D_MODEL)))
        args.append(final_nw.reshape(1, D_MODEL))
        out_specs.append(row(D_MODEL))
        out_shape.append(jax.ShapeDtypeStruct((n, D_MODEL), F32))
    return pl.pallas_call(
        kern,
        grid=(n // tm,),
        in_specs=in_specs,
        out_specs=out_specs,
        out_shape=out_shape,
        compiler_params=pltpu.CompilerParams(
            dimension_semantics=("parallel",), vmem_limit_bytes=VMEM_LIMIT_BYTES),
        name="mix_out_mlp",
    )(*args)


def _rwkv_kernel(p_ref, prev_ref, s0_ref, mu_ref, w0_ref, w2_ref, a0_ref, a2_ref, g2_ref,
                 kk_ref, ka_ref, rk_ref, lnw_ref, lnb_ref, hsum_ref,
                 out_ref, s_out_ref, xp_ref, state_ref, *, c, g, gs, nc):
    ci = pl.program_id(1)
    base = SUBLANES
    gc = g * c

    hd = RWKV_HD

    @pl.when(ci == 0)
    def _():
        for s in range(g):
            xp_ref[s, base - 1:base, :] = prev_ref[s]
        state_ref[...] = jnp.zeros(state_ref.shape, F32)
        for s in range(g):
            for j in range(RWKV_PAIRS):
                state_ref[s, j, 0:hd, 0:hd] = s0_ref[s, 2 * j]
                state_ref[s, j, hd:2 * hd, hd:2 * hd] = s0_ref[s, 2 * j + 1]

    shifted = []
    for s in range(g):
        ps = p_ref[s]
        xp_ref[s, base:base + c, :] = ps
        shifted.append(xp_ref[s, base - 1:base - 1 + c, :])
        xp_ref[s, base - 1:base, :] = ps[c - 1:c, :]
    p = p_ref[...].reshape(gc, RWKV_PROJ)
    shifted = _cat(shifted)

    xs = p + mu_ref[...] * (shifted - p)
    r = xs[:, 0:RWKV_W]
    k = xs[:, RWKV_W:2 * RWKV_W]
    v = xs[:, 2 * RWKV_W:3 * RWKV_W]
    xwa = xs[:, 3 * RWKV_W:3 * RWKV_W + DECAY_LORA + AAA_LORA]
    xg = xs[:, 3 * RWKV_W + DECAY_LORA + AAA_LORA:]

    ld = -RWKV_DECAY_SCALE * _sigmoid(w0_ref[...] + _mm(jnp.tanh(xwa), w2_ref[...]))
    a = _sigmoid(a0_ref[...] + _mm(xwa, a2_ref[...]))
    gate = _mm(_sigmoid(xg), g2_ref[...])

    incl1, _, same1, _ = _block_masks(gc, c)
    sums = _mm_exact_lhs(jnp.concatenate([incl1, same1], axis=0).astype(BF16), ld)
    cum = sums[:gc]
    tot = sums[gc:]
    w_in = jnp.exp(cum)
    w_ex = jnp.exp(cum - ld)
    w_inv = jnp.exp(-cum)
    w_tail = jnp.exp(tot - cum)
    w_last = jnp.exp(tot)

    hsum = hsum_ref[...]

    def head_sums(x):
        return jnp.concatenate(
            [_mm_exact_rhs(x[:, j * LANES:(j + 1) * LANES], hsum) for j in range(RWKV_PAIRS)], axis=1)

    kkr = k * kk_ref[...]
    kk = kkr * lax.rsqrt(head_sums(kkr * kkr) + 1e-6)
    kp = k * (1.0 + (a - 1.0) * ka_ref[...])
    bv = kk * a

    ns = g // gs
    n = gs * c
    nh = RWKV_HEADS
    lo = lax.broadcasted_iota(jnp.int32, (n, LANES), 1) < RWKV_HD

    def units(x):
        out = []
        for st in range(ns):
            for j in range(RWKV_PAIRS):
                slab = x[st * n:(st + 1) * n, j * LANES:(j + 1) * LANES]
                out += [jnp.where(lo, slab, 0.0), jnp.where(lo, 0.0, slab)]
        return out

    a_t = units(-kk * w_ex)
    r_t = units(r * w_in)
    b_t = units(bv * w_inv)
    k_t = units(kp * w_inv)
    v_b = units(v)
    b_w = units(bv * w_tail)
    k_w = units(kp * w_tail)

    us_ = range(ns * nh)
    incl, strict, _, diag = _block_masks(n, c)
    eye = diag.astype(F32)
    ars = [jnp.concatenate([a_t[u], r_t[u]], axis=0) for u in us_]
    gbs = [_mm_nt(ars[u], b_t[u]) for u in us_]
    gks = [_mm_nt(ars[u], k_t[u]) for u in us_]
    m_ab = [jnp.where(strict, x[:n], 0.0) for x in gbs]
    m_ak = [jnp.where(strict, x[:n], 0.0) for x in gks]
    l_rb = [jnp.where(incl, x[n:], 0.0) for x in gbs]
    l_rk = [jnp.where(incl, x[n:], 0.0) for x in gks]
    t_inv = _unit_lower_inverses(m_ab, eye, c)

    sa = [[] for _ in us_]
    sr = [[] for _ in us_]
    for st in range(ns):
        for j in range(RWKV_PAIRS):
            u0 = st * nh + 2 * j
            for sl in range(gs):
                lhs = jnp.concatenate([_rows(x[u0 + i], sl, c) for x in (a_t, r_t) for i in range(2)], axis=0)
                ps = _mm_nt(lhs, state_ref[st * gs + sl, j])
                for i in range(2):
                    sa[u0 + i].append(_rows(ps, i, c))
                    sr[u0 + i].append(_rows(ps, 2 + i, c))
    sa = [_cat(x) for x in sa]
    sr = [_cat(x) for x in sr]

    mv = [_mm(m_ak[u], v_b[u]) for u in us_]
    us = [_mm3(t_inv[u], sa[u] + mv[u]) for u in us_]
    ys = [sr[u] + _mm(l_rb[u], us[u]) + _mm(l_rk[u], v_b[u]) for u in us_]
    y = _cat([jnp.concatenate([ys[st * nh + 2 * j] + ys[st * nh + 2 * j + 1] for j in range(RWKV_PAIRS)], axis=1)
              for st in range(ns)])

    for st in range(ns):
        for j in range(RWKV_PAIRS):
            u0 = st * nh + 2 * j
            for sl in range(gs):
                s = st * gs + sl
                uv = jnp.concatenate([_rows(x[u0 + i], sl, c) for x in (us, v_b) for i in range(2)], axis=0)
                bk = jnp.concatenate([_rows(x[u0 + i], sl, c) for x in (b_w, k_w) for i in range(2)], axis=0)
                decay = w_last[s * c:s * c + 1, j * LANES:(j + 1) * LANES]
                state_ref[s, j] = state_ref[s, j] * decay + _mm_tn(uv, bk)

    inv_hd = 1.0 / RWKV_HD
    sums = head_sums(jnp.concatenate([y, y * y, r * kp * rk_ref[...]], axis=0))
    mean = sums[:gc] * inv_hd
    var = sums[gc:2 * gc] * inv_hd - mean * mean
    yn = (y - mean) * lax.rsqrt(var + RWKV_GN_EPS) * lnw_ref[...] + lnb_ref[...]
    out_ref[...] = ((yn + sums[2 * gc:] * v) * gate).reshape(g, c, RWKV_W)

    @pl.when(ci == nc - 1)
    def _():
        for s in range(g):
            for j in range(RWKV_PAIRS):
                s_out_ref[s, 2 * j] = state_ref[s, j, 0:hd, 0:hd]
                s_out_ref[s, 2 * j + 1] = state_ref[s, j, hd:2 * hd, hd:2 * hd]


def _rwkv_mix(p, prev, s0, layer, prm, c, g, gs):
    b, t, _ = p.shape
    nc = t // c
    assert nc * c == t and c % SUBLANES == 0 and b % g == 0
    kern = functools.partial(_rwkv_kernel, c=c, g=g, gs=gs, nc=nc)
    consts = [prm["mu"], prm["w0"], prm["w2"], prm["a0"], prm["a2"], prm["g2"], prm["k_k"], prm["k_a"],
              prm["r_k"], prm["ln_w"], prm["ln_b"], prm["hsum"]]
    st_block = (g, RWKV_HEADS, RWKV_HD, RWKV_HD)
    return pl.pallas_call(
        kern,
        grid=(b // g, nc),
        in_specs=[
            pl.BlockSpec((g, c, RWKV_PROJ), lambda i, j: (i, j, 0)),
            pl.BlockSpec((g, 1, RWKV_PROJ), lambda i, j: (i, 0, 0)),
            pl.BlockSpec((None,) + st_block, lambda i, j: (layer, i, 0, 0, 0)),
        ] + [_const_spec(x.shape) for x in consts],
        out_specs=[
            pl.BlockSpec((g, c, RWKV_W), lambda i, j: (i, j, 0)),
            pl.BlockSpec(st_block, lambda i, j: (i, 0, 0, 0)),
        ],
        out_shape=[
            jax.ShapeDtypeStruct((b, t, RWKV_W), F32),
            jax.ShapeDtypeStruct((b,) + st_block[1:], F32),
        ],
        scratch_shapes=[
            pltpu.VMEM((g, c + SUBLANES, RWKV_PROJ), F32),
            pltpu.VMEM((g, RWKV_PAIRS, LANES, LANES), F32),
        ],
        compiler_params=pltpu.CompilerParams(
            dimension_semantics=("parallel", "arbitrary"), vmem_limit_bytes=VMEM_LIMIT_BYTES),
        name="rwkv7_mix",
    )(p, prev.reshape(b, 1, RWKV_PROJ), s0, *consts)


def _gdn_kernel(qkv_ref, z_ref, ba_ref, buf_ref, s0_ref, cw_ref, alog_ref, dt_ref, nw_ref,
                out_ref, s_out_ref, xp_ref, state_ref, *, c, g, gs, nc):
    ci = pl.program_id(1)
    base = SUBLANES
    nb = CONV_W - 1
    gc = g * c

    @pl.when(ci == 0)
    def _():
        for s in range(g):
            xp_ref[s, base - nb:base, :] = buf_ref[s]
        state_ref[...] = s0_ref[...]

    qkv = _silu(_cat([_causal_conv(qkv_ref, xp_ref, cw_ref, s, c) for s in range(g)]))
    z = z_ref[...].reshape(gc, GDN_W)
    ba = ba_ref[...].reshape(gc, GDN_BA_PAD)
    beta_all = _sigmoid(ba)
    g_all = -jnp.exp(alog_ref[...]) * _softplus(ba + dt_ref[...])
    incl_all, strict_all, same_all, _ = _block_masks(gc, c)
    sums = _mm_exact_lhs(jnp.concatenate([incl_all, same_all], axis=0).astype(BF16), g_all)
    gc_all = sums[:gc]
    tot_all = sums[gc:]
    upper = (same_all & jnp.logical_not(strict_all)).astype(BF16)
    dn = (((0,), (0,)), ((), ()))
    gct = sum(lax.dot_general(t, upper, dn, preferred_element_type=F32) for t in _split2(g_all))

    ns = g // gs
    n = gs * c
    incl, strict, _, diag = _block_masks(n, c)
    eye = diag.astype(F32)
    stack = lambda x, st: x[st * n:(st + 1) * n]

    hs = range(GDN_HEADS)
    head = lambda x, part, h: x[:, part * GDN_W + h * GDN_HD:part * GDN_W + (h + 1) * GDN_HD]
    q = [head(qkv, 0, h) for h in hs]
    k = [head(qkv, 1, h) for h in hs]
    v = [head(qkv, 2, h) for h in hs]
    q = [x * lax.rsqrt(jnp.sum(x * x, axis=-1, keepdims=True) + 1e-6) * (GDN_HD ** -0.5) for x in q]
    k = [x * lax.rsqrt(jnp.sum(x * x, axis=-1, keepdims=True) + 1e-6) for x in k]
    beta = [beta_all[:, h:h + 1] for h in hs]
    gcol = [gc_all[:, GDN_HEADS + h:GDN_HEADS + h + 1] for h in hs]
    grow = [gct[GDN_HEADS + h:GDN_HEADS + h + 1, :] for h in hs]
    glast = [tot_all[:, GDN_HEADS + h:GDN_HEADS + h + 1] for h in hs]
    kb = [k[h] * beta[h] for h in hs]
    vb = [v[h] * beta[h] for h in hs]
    eg = [jnp.exp(gcol[h]) for h in hs]
    qe = [q[h] * eg[h] for h in hs]
    kbe = [kb[h] * eg[h] for h in hs]
    k_tail = [k[h] * jnp.exp(glast[h] - gcol[h]) for h in hs]

    units = [(st, h) for st in range(ns) for h in hs]
    decay = [jnp.where(incl, jnp.exp(jnp.where(incl, stack(gcol[h], st) - grow[h][:, st * n:(st + 1) * n], 0.0)),
                       0.0) for st, h in units]
    gq = [_mm_nt(jnp.concatenate([stack(q[h], st), stack(kb[h], st)], axis=0), stack(k[h], st))
          for st, h in units]
    qk = [jnp.where(incl, x[:n] * d, 0.0) for x, d in zip(gq, decay)]
    lmat = [jnp.where(strict, x[n:] * d, 0.0) for x, d in zip(gq, decay)]
    t_inv = _unit_lower_inverses([-x for x in lmat], eye, c)
    uw = [_mm3(t, jnp.concatenate([stack(vb[h], st), stack(kbe[h], st)], axis=1))
          for t, (st, h) in zip(t_inv, units)]

    q_s, w_s = [], []
    for u, (st, h) in enumerate(units):
        pieces = []
        for sl in range(gs):
            lhs = jnp.concatenate([_rows(stack(qe[h], st), sl, c), _rows(uw[u][:, GDN_HD:], sl, c)], axis=0)
            pieces.append(_mm(lhs, state_ref[st * gs + sl, h]))
        q_s.append(_cat([x[:c] for x in pieces]))
        w_s.append(_cat([x[c:] for x in pieces]))
    v_new = [uw[u][:, :GDN_HD] - w_s[u] for u in range(len(units))]
    o_u = [q_s[u] + _mm(qk[u], v_new[u]) for u in range(len(units))]
    for u, (st, h) in enumerate(units):
        for sl in range(gs):
            s = st * gs + sl
            decay_s = jnp.exp(glast[h][s * c:s * c + 1, :])
            state_ref[s, h] = (state_ref[s, h] * decay_s
                               + _mm_tn(_rows(stack(k_tail[h], st), sl, c), _rows(v_new[u], sl, c)))
    o = [_cat([o_u[st * GDN_HEADS + h] for st in range(ns)]) for h in hs]
    outs = []
    for h in hs:
        on = o[h] * lax.rsqrt(jnp.mean(o[h] * o[h], axis=-1, keepdims=True) + NORM_EPS) * nw_ref[...]
        outs.append(on * _silu(z[:, h * GDN_HD:(h + 1) * GDN_HD]))
    out_ref[...] = jnp.concatenate(outs, axis=1).reshape(g, c, GDN_W)

    @pl.when(ci == nc - 1)
    def _():
        s_out_ref[...] = state_ref[...]


def _gdn_mix(qkv, z, ba, buf, s0, layer, prm, c, g, gs):
    b, t, _ = qkv.shape
    nc = t // c
    assert nc * c == t and c % SUBLANES == 0 and b % g == 0
    kern = functools.partial(_gdn_kernel, c=c, g=g, gs=gs, nc=nc)
    consts = [prm["conv_w"], prm["a_log"], prm["dt_bias"], prm["norm_w"]]
    st_block = (g, GDN_HEADS, GDN_HD, GDN_HD)
    return pl.pallas_call(
        kern,
        grid=(b // g, nc),
        in_specs=[
            pl.BlockSpec((g, c, GDN_QKV), lambda i, j: (i, j, 0)),
            pl.BlockSpec((g, c, GDN_W), lambda i, j: (i, j, 0)),
            pl.BlockSpec((g, c, GDN_BA_PAD), lambda i, j: (i, j, 0)),
            pl.BlockSpec((g, CONV_W - 1, GDN_QKV), lambda i, j: (i, 0, 0)),
            pl.BlockSpec((None,) + st_block, lambda i, j: (layer, i, 0, 0, 0)),
        ] + [_const_spec(x.shape) for x in consts],
        out_specs=[
            pl.BlockSpec((g, c, GDN_W), lambda i, j: (i, j, 0)),
            pl.BlockSpec(st_block, lambda i, j: (i, 0, 0, 0)),
        ],
        out_shape=[
            jax.ShapeDtypeStruct((b, t, GDN_W), F32),
            jax.ShapeDtypeStruct((b,) + st_block[1:], F32),
        ],
        scratch_shapes=[
            pltpu.VMEM((g, c + SUBLANES, GDN_QKV), F32),
            pltpu.VMEM(st_block, F32),
        ],
        compiler_params=pltpu.CompilerParams(
            dimension_semantics=("parallel", "arbitrary"), vmem_limit_bytes=VMEM_LIMIT_BYTES),
        name="gdn_mix",
    )(qkv, z, ba, buf, s0, *consts)


def _lru_kernel(gate_ref, xb_ref, buf_ref, h0_ref, cw_ref, cb_ref, wab_ref, bab_ref, l_ref,
                out_ref, h_out_ref, xp_ref, carry_ref, sa_ref, sb_ref, *, c, g, nc):
    ci = pl.program_id(1)
    base = SUBLANES
    nb = CONV_W - 1
    gc = g * c

    @pl.when(ci == 0)
    def _():
        for s in range(g):
            xp_ref[s, base - nb:base, :] = buf_ref[s]
        carry_ref[...] = h0_ref[...]

    xc = _cat([_causal_conv(xb_ref, xp_ref, cw_ref, s, c) for s in range(g)]) + cb_ref[...]

    for h in range(LRU_HEADS):
        sl = slice(h * LRU_BW, (h + 1) * LRU_BW)
        xh = xc[:, sl]
        ri = _mm(xh, wab_ref[h]) + bab_ref[h]
        r = _sigmoid(ri[:, :LRU_BW])
        i = _sigmoid(ri[:, LRU_BW:])
        log_a = -LRU_C * r * _softplus(-l_ref[:, sl])
        a = jnp.exp(log_a)
        mult = jnp.sqrt(-jnp.tanh(log_a) * (a * a + 1.0))
        sa_ref[:, sl] = a
        sb_ref[:, sl] = mult * i * xh

    tpos = jnp.bitwise_and(lax.broadcasted_iota(jnp.int32, (gc, 1), 0), SUBLANES - 1)
    a = sa_ref[...]
    b = sb_ref[...]

    def shift_in_group(x, d):
        x3 = x.reshape(gc // SUBLANES, SUBLANES, LRU_W)
        return pltpu.roll(x3, d, axis=1).reshape(gc, LRU_W)

    d = 1
    while d < SUBLANES:
        live = tpos >= d
        a_sh = jnp.where(live, shift_in_group(a, d), 1.0)
        b_sh = jnp.where(live, shift_in_group(b, d), 0.0)
        b = a * b_sh + b
        a = a * a_sh
        d *= 2
    groups = []
    for s in range(g):
        h = carry_ref[s]
        for r0 in range(s * c, (s + 1) * c, SUBLANES):
            hg = a[r0:r0 + SUBLANES] * h + b[r0:r0 + SUBLANES]
            groups.append(hg)
            h = hg[SUBLANES - 1:SUBLANES]
        carry_ref[s] = h
    hs = _cat(groups)
    out_ref[...] = (hs * _gelu_tanh(gate_ref[...].reshape(gc, LRU_W))).reshape(g, c, LRU_W)

    @pl.when(ci == nc - 1)
    def _():
        h_out_ref[...] = carry_ref[...]


def _lru_mix(gate, xb, buf, h0, prm, c, g):
    b, t, _ = xb.shape
    nc = t // c
    assert nc * c == t and c % SUBLANES == 0 and b % g == 0
    kern = functools.partial(_lru_kernel, c=c, g=g, nc=nc)
    consts = [prm["conv_w"], prm["conv_b"], prm["wab"], prm["bab"], prm["l"]]
    seq = lambda rows: pl.BlockSpec((g, rows, LRU_W), lambda i, j: (i, 0, 0))
    chunk = pl.BlockSpec((g, c, LRU_W), lambda i, j: (i, j, 0))
    y, h_last = pl.pallas_call(
        kern,
        grid=(b // g, nc),
        in_specs=[chunk, chunk, seq(CONV_W - 1), seq(1)] + [_const_spec(x.shape) for x in consts],
        out_specs=[chunk, seq(1)],
        out_shape=[
            jax.ShapeDtypeStruct((b, t, LRU_W), F32),
            jax.ShapeDtypeStruct((b, 1, LRU_W), F32),
        ],
        scratch_shapes=[
            pltpu.VMEM((g, c + SUBLANES, LRU_W), F32),
            pltpu.VMEM((g, 1, LRU_W), F32),
            pltpu.VMEM((g * c, LRU_W), F32),
            pltpu.VMEM((g * c, LRU_W), F32),
        ],
        compiler_params=pltpu.CompilerParams(
            dimension_semantics=("parallel", "arbitrary"), vmem_limit_bytes=VMEM_LIMIT_BYTES),
        name="lru_mix",
    )(gate, xb, buf, h0.reshape(b, 1, LRU_W), *consts)
    return y, h_last.reshape(b, LRU_W)


def _prep_even(i, rwkv_mu, rwkv_w0, rwkv_w2, rwkv_a0, rwkv_a2, rwkv_g2, rwkv_k_k,
               rwkv_k_a, rwkv_r_k, rwkv_ln_w, rwkv_ln_b, gdn_conv_w, gdn_A_log, gdn_dt_bias, gdn_norm_w):
    n_ba = 2 * GDN_HEADS
    row = lambda x: x.reshape(1, -1)
    lane = jnp.arange(LANES) // RWKV_HD
    lora = DECAY_LORA + AAA_LORA
    rwkv = dict(
        mu=row(rwkv_mu[i]), w0=row(rwkv_w0[i]), a0=row(rwkv_a0[i]),
        w2=jnp.pad(rwkv_w2[i], ((0, lora - DECAY_LORA), (0, 0))).astype(BF16),
        a2=jnp.pad(rwkv_a2[i], ((DECAY_LORA, 0), (0, 0))).astype(BF16),
        g2=rwkv_g2[i].astype(BF16),
        k_k=row(rwkv_k_k[i]), k_a=row(rwkv_k_a[i]), r_k=row(rwkv_r_k[i]),
        ln_w=row(rwkv_ln_w[i]), ln_b=row(rwkv_ln_b[i]),
        hsum=(lane[:, None] == lane[None, :]).astype(BF16),
    )
    pad_heads = lambda x: jnp.pad(x, (GDN_HEADS, GDN_BA_PAD - n_ba)).reshape(1, GDN_BA_PAD)
    gdn = dict(conv_w=gdn_conv_w[i], a_log=pad_heads(gdn_A_log[i]), dt_bias=pad_heads(gdn_dt_bias[i]),
               norm_w=row(gdn_norm_w[i]))
    return rwkv, gdn


def _prep_odd(i, lru_conv_w, lru_conv_b, lru_wa, lru_ba, lru_wi, lru_bi, lru_L):
    lru = dict(
        conv_w=lru_conv_w[i], conv_b=lru_conv_b[i].reshape(1, LRU_W),
        wab=jnp.concatenate([lru_wa[i], lru_wi[i]], axis=-1).astype(BF16),
        bab=jnp.concatenate([lru_ba[i], lru_bi[i]], axis=-1).reshape(LRU_HEADS, 1, 2 * LRU_BW),
        l=lru_L[i].reshape(1, LRU_W),
    )
    return lru


def _prep_in_proj(w_in_even, w_in_odd):
    main = RWKV_PROJ + GDN_QKV + GDN_W
    n_ba = 2 * GDN_HEADS
    even_ba = jnp.pad(w_in_even[:, :, main:], ((0, 0), (0, 0), (0, GDN_BA_PAD - n_ba)))
    return w_in_even.astype(BF16), even_ba.astype(BF16), w_in_odd.astype(BF16)


def _row_tile(n):
    return 512 if n % 512 == 0 else 256


def _seq_group(b, c, rows):
    g = max(1, min(b, rows // c))
    while b % g:
        g -= 1
    return g


def _trunk(x, st_rwkv, st_shift, st_gdn, st_gconv, st_lru, st_lconv, in_w, out_w, mlp_w, even_p, odd_p,
           norm_mix, norm_mlp, norm_final):
    b, t, _ = x.shape
    n = b * t
    tm = _row_tile(n)
    c_even = min(64, t)
    gs_even = _seq_group(b, c_even, 128)
    n_stacks = max(k for k in (1, 2) if k * gs_even <= MAX_SEQS_PER_STEP and b % (k * gs_even) == 0)
    g_even = gs_even * n_stacks
    c_odd = min(256, t)
    g_odd = _seq_group(b, c_odd, 256)
    h = x.reshape(n, D_MODEL)
    n_rwkv, n_shift, n_gdn, n_gconv, n_lru, n_lconv = [], [], [], [], [], []
    y_final = None
    for l in range(DEPTH):
        i = l // 2
        up, down = mlp_w
        wo = out_w[l % 2]
        final_nw = norm_final if l == DEPTH - 1 else None
        if l % 2 == 0:
            rwkv, gdn = even_p[i]
            p, qkv, z, ba = _norm_proj(h, norm_mix[l], in_w[:2], i,
                                       ((RWKV_PROJ, GDN_QKV, GDN_W), (GDN_BA_PAD,)), tm)
            p = p.reshape(b, t, RWKV_PROJ)
            qkv = qkv.reshape(b, t, GDN_QKV)
            ya, s_a = _rwkv_mix(p, st_shift[i], st_rwkv, i, rwkv, c_even, g_even, gs_even)
            yb, s_b = _gdn_mix(qkv, z.reshape(b, t, GDN_W), ba.reshape(b, t, GDN_BA_PAD), st_gconv[i],
                               st_gdn, i, gdn, c_even, g_even, gs_even)
            n_rwkv.append(s_a)
            n_shift.append(p[:, -1])
            n_gdn.append(s_b)
            n_gconv.append(qkv[:, t - (CONV_W - 1):])
            ys = (ya.reshape(n, RWKV_W), yb.reshape(n, GDN_W))
        else:
            lru = odd_p[i]
            gate, xb = _norm_proj(h, norm_mix[l], in_w[2:], i, ((LRU_W, LRU_W),), tm)
            xb = xb.reshape(b, t, LRU_W)
            yc, h_last = _lru_mix(gate.reshape(b, t, LRU_W), xb, st_lconv[i], st_lru[i], lru, c_odd, g_odd)
            n_lru.append(h_last)
            n_lconv.append(xb[:, t - (CONV_W - 1):])
            ys = (yc.reshape(n, LRU_W),)
        outs = _mix_out_mlp(h, ys, wo, i, norm_mlp[l], up, down, l, final_nw, tm)
        h = outs[0]
        if final_nw is not None:
            y_final = outs[1]
    return (y_final.reshape(b, t, D_MODEL), jnp.stack(n_rwkv), jnp.stack(n_shift), jnp.stack(n_gdn),
            jnp.stack(n_gconv), jnp.stack(n_lru), jnp.stack(n_lconv))


def kernel(x_prompt, x_sample, state_rwkv, state_rwkv_shift, state_gdn, state_gdn_conv, state_lru, state_lru_conv, norm_mix, norm_mlp, norm_final, w_in_even, w_out_even, rwkv_mu, rwkv_w0, rwkv_w2, rwkv_a0, rwkv_a2, rwkv_g2, rwkv_k_k, rwkv_k_a, rwkv_r_k, rwkv_ln_w, rwkv_ln_b, gdn_conv_w, gdn_A_log, gdn_dt_bias, gdn_norm_w, w_in_odd, w_out_odd, lru_conv_w, lru_conv_b, lru_wa, lru_ba, lru_wi, lru_bi, lru_L, mlp_up, mlp_down):
    n_even = (DEPTH + 1) // 2
    n_odd = DEPTH // 2
    even_p = [_prep_even(i, rwkv_mu, rwkv_w0, rwkv_w2, rwkv_a0, rwkv_a2, rwkv_g2,
                         rwkv_k_k, rwkv_k_a, rwkv_r_k, rwkv_ln_w, rwkv_ln_b, gdn_conv_w, gdn_A_log,
                         gdn_dt_bias, gdn_norm_w) for i in range(n_even)]
    odd_p = [_prep_odd(i, lru_conv_w, lru_conv_b, lru_wa, lru_ba, lru_wi, lru_bi, lru_L)
             for i in range(n_odd)]
    mlp_w = (mlp_up.astype(BF16), mlp_down.astype(BF16))
    out_w = (w_out_even.astype(BF16), w_out_odd.astype(BF16))
    bp = x_prompt.shape[0]
    dt = x_prompt.dtype
    zeros = lambda *s: jnp.zeros(s, dt)
    shared = (_prep_in_proj(w_in_even, w_in_odd), out_w, mlp_w, even_p, odd_p, norm_mix, norm_mlp, norm_final)
    y_p, rwkv_p, shift_p, gdn_p, gconv_p, lru_p, lconv_p = _trunk(
        x_prompt,
        zeros(n_even, bp, RWKV_HEADS, RWKV_HD, RWKV_HD), zeros(n_even, bp, RWKV_PROJ),
        zeros(n_even, bp, GDN_HEADS, GDN_HD, GDN_HD), zeros(n_even, bp, CONV_W - 1, GDN_QKV),
        zeros(n_odd, bp, LRU_W), zeros(n_odd, bp, CONV_W - 1, LRU_W), *shared)
    y_s, rwkv_s, shift_s, gdn_s, gconv_s, lru_s, lconv_s = _trunk(
        x_sample, state_rwkv, state_rwkv_shift, state_gdn, state_gdn_conv, state_lru, state_lru_conv, *shared)
    return (y_p, y_s, rwkv_p, rwkv_s, shift_p, shift_s, gdn_p, gdn_s, gconv_p, gconv_s, lru_p, lru_s,
            lconv_p, lconv_s)
```

```python
import functools

import jax
import jax.numpy as jnp
from jax import lax
from jax.experimental import pallas as pl
from jax.experimental.pallas import tpu as pltpu

F32 = jnp.float32
BF16 = jnp.bfloat16

D_MODEL = 1024
DEPTH = 4
CONV_W = 4
NORM_EPS = 1e-6
RWKV_HEADS = 8
RWKV_HD = 64
RWKV_W = RWKV_HEADS * RWKV_HD
RWKV_PAIRS = RWKV_HEADS // 2
DECAY_LORA = 64
AAA_LORA = 64
GATE_LORA = 128
RWKV_PROJ = 3 * RWKV_W + DECAY_LORA + AAA_LORA + GATE_LORA
RWKV_GN_EPS = 64e-5
RWKV_DECAY_SCALE = 0.6065306597126334
GDN_HEADS = 4
GDN_HD = 128
GDN_W = GDN_HEADS * GDN_HD
GDN_QKV = 3 * GDN_W
GDN_BA_PAD = 128
LRU_W = D_MODEL
LRU_HEADS = 8
LRU_BW = LRU_W // LRU_HEADS
LRU_C = 8.0
D_FF = 4 * D_MODEL

SUBLANES = 8
LANES = 128
VMEM_LIMIT_BYTES = 56 * 1024 * 1024
MAX_SEQS_PER_STEP = 16


def _mm(a, b):
    return jnp.dot(a.astype(BF16), b.astype(BF16), preferred_element_type=F32)


def _mm_nt(a, b):
    return lax.dot_general(a.astype(BF16), b.astype(BF16), (((1,), (1,)), ((), ())),
                           preferred_element_type=F32)


def _mm_tn(a, b):
    return lax.dot_general(a.astype(BF16), b.astype(BF16), (((0,), (0,)), ((), ())),
                           preferred_element_type=F32)


def _split2(x):
    hi = x.astype(BF16)
    return hi, (x - hi.astype(F32)).astype(BF16)


def _mm_exact_lhs(m01, x):
    hi, lo = _split2(x)
    d = lambda t: jnp.dot(m01, t, preferred_element_type=F32)
    return d(hi) + d(lo)


def _mm_exact_rhs(x, m01):
    hi, lo = _split2(x)
    d = lambda t: jnp.dot(t, m01, preferred_element_type=F32)
    return d(hi) + d(lo)


def _mm3(a, b):
    ah, al = _split2(a)
    bh, bl = _split2(b)
    d = lambda x, y: jnp.dot(x, y, preferred_element_type=F32)
    return d(ah, bh) + d(ah, bl) + d(al, bh)


def _softplus(x):
    return jnp.maximum(x, 0.0) + jnp.log1p(jnp.exp(-jnp.abs(x)))


def _sigmoid(x):
    return jax.nn.sigmoid(x)


def _silu(x):
    return x * jax.nn.sigmoid(x)


def _gelu_tanh(x):
    return 0.5 * x * (1.0 + jnp.tanh(0.7978845608028654 * (x + 0.044715 * (x * x * x))))


def _block_masks(n, c):
    row = lax.broadcasted_iota(jnp.int32, (n, n), 0)
    col = lax.broadcasted_iota(jnp.int32, (n, n), 1)
    same = jnp.bitwise_xor(row, col) < c
    return same & (row >= col), same & (row > col), same, row == col


def _unit_lower_inverses(ms, eye, c):
    n = eye.shape[0]
    row = lax.broadcasted_iota(jnp.int32, (n, n), 0)
    col = lax.broadcasted_iota(jnp.int32, (n, n), 1)
    corner = lambda s: ((jnp.bitwise_xor(row, col) < 2 * s) & (jnp.bitwise_and(row, s) != 0)
                        & (jnp.bitwise_and(col, s) == 0))
    ps = [eye + jnp.where(corner(1), m, 0.0) for m in ms]
    s = 2
    while s < c:
        mask = corner(s)
        pb = [_mm(p, jnp.where(mask, m, 0.0)) for p, m in zip(ps, ms)]
        ps = [p + _mm(x, p) for p, x in zip(ps, pb)]
        s *= 2
    return ps


def _rows(x, s, c):
    return x[s * c:(s + 1) * c]


def _cat(xs, axis=0):
    return xs[0] if len(xs) == 1 else jnp.concatenate(xs, axis=axis)


def _const_spec(shape):
    nd = len(shape)
    return pl.BlockSpec(shape, lambda *_: (0,) * nd)


def _skip_first_ref(kern, _, *refs):
    return kern(*refs)


def _in_place_slot(kern, s_prev, out_index):
    if s_prev is None:
        return kern, [], [], {}
    return (functools.partial(_skip_first_ref, kern), [pl.BlockSpec(memory_space=pl.ANY)], [s_prev],
            {0: out_index})


def _causal_conv(x_ref, xp_ref, cw_ref, s, c):
    base = SUBLANES
    nb = CONV_W - 1
    xp_ref[s, base:base + c, :] = x_ref[s]
    y = cw_ref[0:1, :] * xp_ref[s, base - nb:base - nb + c, :]
    for j in range(1, CONV_W):
        y = y + cw_ref[j:j + 1, :] * xp_ref[s, base - nb + j:base - nb + j + c, :]
    xp_ref[s, base - nb:base, :] = xp_ref[s, base + c - nb:base + c, :]
    return y


def _norm_proj_kernel(x_ref, nw_ref, *refs, groups, col_chunk):
    n_w = len(groups)
    w_refs, o_refs = refs[:n_w], refs[n_w:]
    x = x_ref[...]
    var = jnp.mean(x * x, axis=-1, keepdims=True)
    xn = (x * lax.rsqrt(var + NORM_EPS) * nw_ref[...]).astype(BF16)
    o_iter = iter(o_refs)
    for w_ref, widths in zip(w_refs, groups):
        off = 0
        for width in widths:
            o_ref = next(o_iter)
            for c0 in range(0, width, col_chunk):
                cw = min(col_chunk, width - c0)
                o_ref[:, c0:c0 + cw] = jnp.dot(xn, w_ref[:, off + c0:off + c0 + cw],
                                               preferred_element_type=F32)
            off += width


def _norm_proj(x, nw, ws, layer, groups, tm):
    n = x.shape[0]
    assert n % tm == 0 and all(sum(gr) <= w.shape[2] for gr, w in zip(groups, ws))
    kern = functools.partial(_norm_proj_kernel, groups=tuple(tuple(gr) for gr in groups), col_chunk=512)
    widths = [wd for gr in groups for wd in gr]
    return pl.pallas_call(
        kern,
        grid=(n // tm,),
        in_specs=[
            pl.BlockSpec((tm, D_MODEL), lambda i: (i, 0)),
            _const_spec((1, D_MODEL)),
        ] + [pl.BlockSpec((None, D_MODEL, w.shape[2]), lambda i: (layer, 0, 0)) for w in ws],
        out_specs=[pl.BlockSpec((tm, wd), lambda i: (i, 0)) for wd in widths],
        out_shape=[jax.ShapeDtypeStruct((n, wd), F32) for wd in widths],
        compiler_params=pltpu.CompilerParams(
            dimension_semantics=("parallel",), vmem_limit_bytes=VMEM_LIMIT_BYTES),
        name="norm_proj",
    )(x, nw.reshape(1, D_MODEL), *ws)


def _mlp_kernel(*refs, n_mix, ff_chunk, final):
    h_ref = refs[0]
    y_refs = refs[1:1 + n_mix]
    wo_ref, nw_ref, up_ref, down_ref = refs[1 + n_mix:5 + n_mix]
    rest = refs[5 + n_mix:]
    if final:
        fnw_ref, o_ref, yf_ref = rest
    else:
        (o_ref,) = rest
    y = _cat([y_ref[...].astype(BF16) for y_ref in y_refs], axis=1)
    h = h_ref[...] + jnp.dot(y, wo_ref[...], preferred_element_type=F32)
    var = jnp.mean(h * h, axis=-1, keepdims=True)
    xn = (h * lax.rsqrt(var + NORM_EPS) * nw_ref[...]).astype(BF16)
    acc = h
    for c0 in range(0, D_FF, ff_chunk):
        u = jnp.dot(xn, up_ref[:, c0:c0 + ff_chunk], preferred_element_type=F32)
        r = jnp.square(jnp.maximum(u, 0.0)).astype(BF16)
        acc = acc + jnp.dot(r, down_ref[c0:c0 + ff_chunk, :], preferred_element_type=F32)
    o_ref[...] = acc
    if final:
        var2 = jnp.mean(acc * acc, axis=-1, keepdims=True)
        yf_ref[...] = acc * lax.rsqrt(var2 + NORM_EPS) * fnw_ref[...]


def _mix_out_mlp(h, ys, wo, wo_layer, nw, up, down, layer, final_nw, tm):
    n = h.shape[0]
    n_mix = len(ys)
    final = final_nw is not None
    kern = functools.partial(_mlp_kernel, n_mix=n_mix, ff_chunk=512, final=final)
    row = lambda wd: pl.BlockSpec((tm, wd), lambda i: (i, 0))
    pick = lambda w, l: pl.BlockSpec((None,) + w.shape[1:], lambda i: (l, 0, 0))
    in_specs = [row(D_MODEL)] + [row(y.shape[1]) for y in ys]
    in_specs += [pick(wo, wo_layer), _const_spec((1, D_MODEL)), pick(up, layer), pick(down, layer)]
    args = [h, *ys, wo, nw.reshape(1, D_MODEL), up, down]
    out_specs = [row(D_MODEL)]
    out_shape = [jax.ShapeDtypeStruct((n, D_MODEL), F32)]
    if final:
        in_specs.append(_const_spec((1, D_MODEL)))
        args.append(final_nw.reshape(1, D_MODEL))
        out_specs.append(row(D_MODEL))
        out_shape.append(jax.ShapeDtypeStruct((n, D_MODEL), F32))
    return pl.pallas_call(
        kern,
        grid=(n // tm,),
        in_specs=in_specs,
        out_specs=out_specs,
        out_shape=out_shape,
        compiler_params=pltpu.CompilerParams(
            dimension_semantics=("parallel",), vmem_limit_bytes=VMEM_LIMIT_BYTES),
        name="mix_out_mlp",
    )(*args)


def _rwkv_kernel(p_ref, prev_ref, s0_ref, mu_ref, w0_ref, w2_ref, a0_ref, a2_ref, g2_ref,
                 kk_ref, ka_ref, rk_ref, lnw_ref, lnb_ref, hsum_ref,
                 out_ref, s_out_ref, xp_ref, state_ref, *, c, g, gs, nc):
    ci = pl.program_id(1)
    base = SUBLANES
    gc = g * c

    hd = RWKV_HD

    @pl.when(ci == 0)
    def _():
        for s in range(g):
            xp_ref[s, base - 1:base, :] = prev_ref[s]
        state_ref[...] = jnp.zeros(state_ref.shape, F32)
        for s in range(g):
            for j in range(RWKV_PAIRS):
                state_ref[s, j, 0:hd, 0:hd] = s0_ref[s, 2 * j]
                state_ref[s, j, hd:2 * hd, hd:2 * hd] = s0_ref[s, 2 * j + 1]

    shifted = []
    for s in range(g):
        ps = p_ref[s]
        xp_ref[s, base:base + c, :] = ps
        shifted.append(xp_ref[s, base - 1:base - 1 + c, :])
        xp_ref[s, base - 1:base, :] = ps[c - 1:c, :]
    p = p_ref[...].reshape(gc, RWKV_PROJ)
    shifted = _cat(shifted)

    xs = p + mu_ref[...] * (shifted - p)
    r = xs[:, 0:RWKV_W]
    k = xs[:, RWKV_W:2 * RWKV_W]
    v = xs[:, 2 * RWKV_W:3 * RWKV_W]
    xwa = xs[:, 3 * RWKV_W:3 * RWKV_W + DECAY_LORA + AAA_LORA]
    xg = xs[:, 3 * RWKV_W + DECAY_LORA + AAA_LORA:]

    ld = -RWKV_DECAY_SCALE * _sigmoid(w0_ref[...] + _mm(jnp.tanh(xwa), w2_ref[...]))
    a = _sigmoid(a0_ref[...] + _mm(xwa, a2_ref[...]))
    gate = _mm(_sigmoid(xg), g2_ref[...])

    incl1, _, same1, _ = _block_masks(gc, c)
    sums = _mm_exact_lhs(jnp.concatenate([incl1, same1], axis=0).astype(BF16), ld)
    cum = sums[:gc]
    tot = sums[gc:]
    w_in = jnp.exp(cum)
    w_ex = jnp.exp(cum - ld)
    w_inv = jnp.exp(-cum)
    w_tail = jnp.exp(tot - cum)
    w_last = jnp.exp(tot)

    hsum = hsum_ref[...]

    def head_sums(x):
        return jnp.concatenate(
            [_mm_exact_rhs(x[:, j * LANES:(j + 1) * LANES], hsum) for j in range(RWKV_PAIRS)], axis=1)

    kkr = k * kk_ref[...]
    kk = kkr * lax.rsqrt(head_sums(kkr * kkr) + 1e-6)
    kp = k * (1.0 + (a - 1.0) * ka_ref[...])
    bv = kk * a

    ns = g // gs
    n = gs * c
    nh = RWKV_HEADS
    lo = lax.broadcasted_iota(jnp.int32, (n, LANES), 1) < RWKV_HD

    def units(x):
        out = []
        for st in range(ns):
            for j in range(RWKV_PAIRS):
                slab = x[st * n:(st + 1) * n, j * LANES:(j + 1) * LANES]
                out += [jnp.where(lo, slab, 0.0), jnp.where(lo, 0.0, slab)]
        return out

    a_t = units(-kk * w_ex)
    r_t = units(r * w_in)
    b_t = units(bv * w_inv)
    k_t = units(kp * w_inv)
    v_b = units(v)
    b_w = units(bv * w_tail)
    k_w = units(kp * w_tail)

    us_ = range(ns * nh)
    incl, strict, _, diag = _block_masks(n, c)
    eye = diag.astype(F32)
    ars = [jnp.concatenate([a_t[u], r_t[u]], axis=0) for u in us_]
    gbs = [_mm_nt(ars[u], b_t[u]) for u in us_]
    gks = [_mm_nt(ars[u], k_t[u]) for u in us_]
    m_ab = [jnp.where(strict, x[:n], 0.0) for x in gbs]
    m_ak = [jnp.where(strict, x[:n], 0.0) for x in gks]
    l_rb = [jnp.where(incl, x[n:], 0.0) for x in gbs]
    l_rk = [jnp.where(incl, x[n:], 0.0) for x in gks]
    t_inv = _unit_lower_inverses(m_ab, eye, c)

    sa = [[] for _ in us_]
    sr = [[] for _ in us_]
    for st in range(ns):
        for j in range(RWKV_PAIRS):
            u0 = st * nh + 2 * j
            for sl in range(gs):
                lhs = jnp.concatenate([_rows(x[u0 + i], sl, c) for x in (a_t, r_t) for i in range(2)], axis=0)
                ps = _mm_nt(lhs, state_ref[st * gs + sl, j])
                for i in range(2):
                    sa[u0 + i].append(_rows(ps, i, c))
                    sr[u0 + i].append(_rows(ps, 2 + i, c))
    sa = [_cat(x) for x in sa]
    sr = [_cat(x) for x in sr]

    mv = [_mm(m_ak[u], v_b[u]) for u in us_]
    us = [_mm3(t_inv[u], sa[u] + mv[u]) for u in us_]
    ys = [sr[u] + _mm(l_rb[u], us[u]) + _mm(l_rk[u], v_b[u]) for u in us_]
    y = _cat([jnp.concatenate([ys[st * nh + 2 * j] + ys[st * nh + 2 * j + 1] for j in range(RWKV_PAIRS)], axis=1)
              for st in range(ns)])

    for st in range(ns):
        for j in range(RWKV_PAIRS):
            u0 = st * nh + 2 * j
            for sl in range(gs):
                s = st * gs + sl
                uv = jnp.concatenate([_rows(x[u0 + i], sl, c) for x in (us, v_b) for i in range(2)], axis=0)
                bk = jnp.concatenate([_rows(x[u0 + i], sl, c) for x in (b_w, k_w) for i in range(2)], axis=0)
                decay = w_last[s * c:s * c + 1, j * LANES:(j + 1) * LANES]
                state_ref[s, j] = state_ref[s, j] * decay + _mm_tn(uv, bk)

    inv_hd = 1.0 / RWKV_HD
    sums = head_sums(jnp.concatenate([y, y * y, r * kp * rk_ref[...]], axis=0))
    mean = sums[:gc] * inv_hd
    var = sums[gc:2 * gc] * inv_hd - mean * mean
    yn = (y - mean) * lax.rsqrt(var + RWKV_GN_EPS) * lnw_ref[...] + lnb_ref[...]
    out_ref[...] = ((yn + sums[2 * gc:] * v) * gate).reshape(g, c, RWKV_W)

    @pl.when(ci == nc - 1)
    def _():
        for s in range(g):
            for j in range(RWKV_PAIRS):
                s_out_ref[s, 2 * j] = state_ref[s, j, 0:hd, 0:hd]
                s_out_ref[s, 2 * j + 1] = state_ref[s, j, hd:2 * hd, hd:2 * hd]


def _rwkv_mix(p, prev, s0, s_prev, layer, prm, c, g, gs):
    b, t, _ = p.shape
    nc = t // c
    assert nc * c == t and c % SUBLANES == 0 and b % g == 0
    kern = functools.partial(_rwkv_kernel, c=c, g=g, gs=gs, nc=nc)
    kern, prev_specs, prev_args, aliases = _in_place_slot(kern, s_prev, out_index=1)
    consts = [prm["mu"], prm["w0"], prm["w2"], prm["a0"], prm["a2"], prm["g2"], prm["k_k"], prm["k_a"],
              prm["r_k"], prm["ln_w"], prm["ln_b"], prm["hsum"]]
    st_block = (g, RWKV_HEADS, RWKV_HD, RWKV_HD)
    return pl.pallas_call(
        kern,
        grid=(b // g, nc),
        in_specs=prev_specs + [
            pl.BlockSpec((g, c, RWKV_PROJ), lambda i, j: (i, j, 0)),
            pl.BlockSpec((g, 1, RWKV_PROJ), lambda i, j: (i, 0, 0)),
            pl.BlockSpec((None,) + st_block, lambda i, j: (layer, i, 0, 0, 0)),
        ] + [_const_spec(x.shape) for x in consts],
        out_specs=[
            pl.BlockSpec((g, c, RWKV_W), lambda i, j: (i, j, 0)),
            pl.BlockSpec((None,) + st_block, lambda i, j: (layer, i, 0, 0, 0)),
        ],
        out_shape=[
            jax.ShapeDtypeStruct((b, t, RWKV_W), F32),
            jax.ShapeDtypeStruct(s0.shape, F32),
        ],
        input_output_aliases=aliases,
        scratch_shapes=[
            pltpu.VMEM((g, c + SUBLANES, RWKV_PROJ), F32),
            pltpu.VMEM((g, RWKV_PAIRS, LANES, LANES), F32),
        ],
        compiler_params=pltpu.CompilerParams(
            dimension_semantics=("parallel", "arbitrary"), vmem_limit_bytes=VMEM_LIMIT_BYTES),
        name="rwkv7_mix",
    )(*prev_args, p, prev.reshape(b, 1, RWKV_PROJ), s0, *consts)


def _gdn_kernel(qkv_ref, z_ref, ba_ref, buf_ref, s0_ref, cw_ref, alog_ref, dt_ref, nw_ref,
                out_ref, s_out_ref, xp_ref, state_ref, *, c, g, gs, nc):
    ci = pl.program_id(1)
    base = SUBLANES
    nb = CONV_W - 1
    gc = g * c

    @pl.when(ci == 0)
    def _():
        for s in range(g):
            xp_ref[s, base - nb:base, :] = buf_ref[s]
        state_ref[...] = s0_ref[...]

    qkv = _silu(_cat([_causal_conv(qkv_ref, xp_ref, cw_ref, s, c) for s in range(g)]))
    z = z_ref[...].reshape(gc, GDN_W)
    ba = ba_ref[...].reshape(gc, GDN_BA_PAD)
    beta_all = _sigmoid(ba)
    g_all = -jnp.exp(alog_ref[...]) * _softplus(ba + dt_ref[...])
    incl_all, strict_all, same_all, _ = _block_masks(gc, c)
    sums = _mm_exact_lhs(jnp.concatenate([incl_all, same_all], axis=0).astype(BF16), g_all)
    gc_all = sums[:gc]
    tot_all = sums[gc:]
    upper = (same_all & jnp.logical_not(strict_all)).astype(BF16)
    dn = (((0,), (0,)), ((), ()))
    gct = sum(lax.dot_general(t, upper, dn, preferred_element_type=F32) for t in _split2(g_all))

    ns = g // gs
    n = gs * c
    incl, strict, _, diag = _block_masks(n, c)
    eye = diag.astype(F32)
    stack = lambda x, st: x[st * n:(st + 1) * n]

    hs = range(GDN_HEADS)
    head = lambda x, part, h: x[:, part * GDN_W + h * GDN_HD:part * GDN_W + (h + 1) * GDN_HD]
    q = [head(qkv, 0, h) for h in hs]
    k = [head(qkv, 1, h) for h in hs]
    v = [head(qkv, 2, h) for h in hs]
    q = [x * lax.rsqrt(jnp.sum(x * x, axis=-1, keepdims=True) + 1e-6) * (GDN_HD ** -0.5) for x in q]
    k = [x * lax.rsqrt(jnp.sum(x * x, axis=-1, keepdims=True) + 1e-6) for x in k]
    beta = [beta_all[:, h:h + 1] for h in hs]
    gcol = [gc_all[:, GDN_HEADS + h:GDN_HEADS + h + 1] for h in hs]
    grow = [gct[GDN_HEADS + h:GDN_HEADS + h + 1, :] for h in hs]
    glast = [tot_all[:, GDN_HEADS + h:GDN_HEADS + h + 1] for h in hs]
    kb = [k[h] * beta[h] for h in hs]
    vb = [v[h] * beta[h] for h in hs]
    eg = [jnp.exp(gcol[h]) for h in hs]
    qe = [q[h] * eg[h] for h in hs]
    kbe = [kb[h] * eg[h] for h in hs]
    k_tail = [k[h] * jnp.exp(glast[h] - gcol[h]) for h in hs]

    units = [(st, h) for st in range(ns) for h in hs]
    decay = [jnp.where(incl, jnp.exp(jnp.where(incl, stack(gcol[h], st) - grow[h][:, st * n:(st + 1) * n], 0.0)),
                       0.0) for st, h in units]
    gq = [_mm_nt(jnp.concatenate([stack(q[h], st), stack(kb[h], st)], axis=0), stack(k[h], st))
          for st, h in units]
    qk = [jnp.where(incl, x[:n] * d, 0.0) for x, d in zip(gq, decay)]
    lmat = [jnp.where(strict, x[n:] * d, 0.0) for x, d in zip(gq, decay)]
    t_inv = _unit_lower_inverses([-x for x in lmat], eye, c)
    uw = [_mm3(t, jnp.concatenate([stack(vb[h], st), stack(kbe[h], st)], axis=1))
          for t, (st, h) in zip(t_inv, units)]

    q_s, w_s = [], []
    for u, (st, h) in enumerate(units):
        pieces = []
        for sl in range(gs):
            lhs = jnp.concatenate([_rows(stack(qe[h], st), sl, c), _rows(uw[u][:, GDN_HD:], sl, c)], axis=0)
            pieces.append(_mm(lhs, state_ref[st * gs + sl, h]))
        q_s.append(_cat([x[:c] for x in pieces]))
        w_s.append(_cat([x[c:] for x in pieces]))
    v_new = [uw[u][:, :GDN_HD] - w_s[u] for u in range(len(units))]
    o_u = [q_s[u] + _mm(qk[u], v_new[u]) for u in range(len(units))]
    for u, (st, h) in enumerate(units):
        for sl in range(gs):
            s = st * gs + sl
            decay_s = jnp.exp(glast[h][s * c:s * c + 1, :])
            state_ref[s, h] = (state_ref[s, h] * decay_s
                               + _mm_tn(_rows(stack(k_tail[h], st), sl, c), _rows(v_new[u], sl, c)))
    o = [_cat([o_u[st * GDN_HEADS + h] for st in range(ns)]) for h in hs]
    outs = []
    for h in hs:
        on = o[h] * lax.rsqrt(jnp.mean(o[h] * o[h], axis=-1, keepdims=True) + NORM_EPS) * nw_ref[...]
        outs.append(on * _silu(z[:, h * GDN_HD:(h + 1) * GDN_HD]))
    out_ref[...] = jnp.concatenate(outs, axis=1).reshape(g, c, GDN_W)

    @pl.when(ci == nc - 1)
    def _():
        s_out_ref[...] = state_ref[...]


def _gdn_mix(qkv, z, ba, buf, s0, s_prev, layer, prm, c, g, gs):
    b, t, _ = qkv.shape
    nc = t // c
    assert nc * c == t and c % SUBLANES == 0 and b % g == 0
    kern = functools.partial(_gdn_kernel, c=c, g=g, gs=gs, nc=nc)
    kern, prev_specs, prev_args, aliases = _in_place_slot(kern, s_prev, out_index=1)
    consts = [prm["conv_w"], prm["a_log"], prm["dt_bias"], prm["norm_w"]]
    st_block = (g, GDN_HEADS, GDN_HD, GDN_HD)
    return pl.pallas_call(
        kern,
        grid=(b // g, nc),
        in_specs=prev_specs + [
            pl.BlockSpec((g, c, GDN_QKV), lambda i, j: (i, j, 0)),
            pl.BlockSpec((g, c, GDN_W), lambda i, j: (i, j, 0)),
            pl.BlockSpec((g, c, GDN_BA_PAD), lambda i, j: (i, j, 0)),
            pl.BlockSpec((g, CONV_W - 1, GDN_QKV), lambda i, j: (i, 0, 0)),
            pl.BlockSpec((None,) + st_block, lambda i, j: (layer, i, 0, 0, 0)),
        ] + [_const_spec(x.shape) for x in consts],
        out_specs=[
            pl.BlockSpec((g, c, GDN_W), lambda i, j: (i, j, 0)),
            pl.BlockSpec((None,) + st_block, lambda i, j: (layer, i, 0, 0, 0)),
        ],
        out_shape=[
            jax.ShapeDtypeStruct((b, t, GDN_W), F32),
            jax.ShapeDtypeStruct(s0.shape, F32),
        ],
        input_output_aliases=aliases,
        scratch_shapes=[
            pltpu.VMEM((g, c + SUBLANES, GDN_QKV), F32),
            pltpu.VMEM(st_block, F32),
        ],
        compiler_params=pltpu.CompilerParams(
            dimension_semantics=("parallel", "arbitrary"), vmem_limit_bytes=VMEM_LIMIT_BYTES),
        name="gdn_mix",
    )(*prev_args, qkv, z, ba, buf, s0, *consts)


def _lru_kernel(gate_ref, xb_ref, buf_ref, h0_ref, cw_ref, cb_ref, wab_ref, bab_ref, l_ref,
                out_ref, h_out_ref, xp_ref, carry_ref, sa_ref, sb_ref, *, c, g, nc):
    ci = pl.program_id(1)
    base = SUBLANES
    nb = CONV_W - 1
    gc = g * c

    @pl.when(ci == 0)
    def _():
        for s in range(g):
            xp_ref[s, base - nb:base, :] = buf_ref[s]
        carry_ref[...] = h0_ref[...]

    xc = _cat([_causal_conv(xb_ref, xp_ref, cw_ref, s, c) for s in range(g)]) + cb_ref[...]

    for h in range(LRU_HEADS):
        sl = slice(h * LRU_BW, (h + 1) * LRU_BW)
        xh = xc[:, sl]
        ri = _mm(xh, wab_ref[h]) + bab_ref[h]
        r = _sigmoid(ri[:, :LRU_BW])
        i = _sigmoid(ri[:, LRU_BW:])
        log_a = -LRU_C * r * _softplus(-l_ref[:, sl])
        a = jnp.exp(log_a)
        mult = jnp.sqrt(-jnp.tanh(log_a) * (a * a + 1.0))
        sa_ref[:, sl] = a
        sb_ref[:, sl] = mult * i * xh

    tpos = jnp.bitwise_and(lax.broadcasted_iota(jnp.int32, (gc, 1), 0), SUBLANES - 1)
    a = sa_ref[...]
    b = sb_ref[...]

    def shift_in_group(x, d):
        x3 = x.reshape(gc // SUBLANES, SUBLANES, LRU_W)
        return pltpu.roll(x3, d, axis=1).reshape(gc, LRU_W)

    d = 1
    while d < SUBLANES:
        live = tpos >= d
        a_sh = jnp.where(live, shift_in_group(a, d), 1.0)
        b_sh = jnp.where(live, shift_in_group(b, d), 0.0)
        b = a * b_sh + b
        a = a * a_sh
        d *= 2
    groups = []
    for s in range(g):
        h = carry_ref[s]
        for r0 in range(s * c, (s + 1) * c, SUBLANES):
            hg = a[r0:r0 + SUBLANES] * h + b[r0:r0 + SUBLANES]
            groups.append(hg)
            h = hg[SUBLANES - 1:SUBLANES]
        carry_ref[s] = h
    hs = _cat(groups)
    out_ref[...] = (hs * _gelu_tanh(gate_ref[...].reshape(gc, LRU_W))).reshape(g, c, LRU_W)

    @pl.when(ci == nc - 1)
    def _():
        h_out_ref[...] = carry_ref[...]


def _lru_mix(gate, xb, buf, h0, prm, c, g):
    b, t, _ = xb.shape
    nc = t // c
    assert nc * c == t and c % SUBLANES == 0 and b % g == 0
    kern = functools.partial(_lru_kernel, c=c, g=g, nc=nc)
    consts = [prm["conv_w"], prm["conv_b"], prm["wab"], prm["bab"], prm["l"]]
    seq = lambda rows: pl.BlockSpec((g, rows, LRU_W), lambda i, j: (i, 0, 0))
    chunk = pl.BlockSpec((g, c, LRU_W), lambda i, j: (i, j, 0))
    y, h_last = pl.pallas_call(
        kern,
        grid=(b // g, nc),
        in_specs=[chunk, chunk, seq(CONV_W - 1), seq(1)] + [_const_spec(x.shape) for x in consts],
        out_specs=[chunk, seq(1)],
        out_shape=[
            jax.ShapeDtypeStruct((b, t, LRU_W), F32),
            jax.ShapeDtypeStruct((b, 1, LRU_W), F32),
        ],
        scratch_shapes=[
            pltpu.VMEM((g, c + SUBLANES, LRU_W), F32),
            pltpu.VMEM((g, 1, LRU_W), F32),
            pltpu.VMEM((g * c, LRU_W), F32),
            pltpu.VMEM((g * c, LRU_W), F32),
        ],
        compiler_params=pltpu.CompilerParams(
            dimension_semantics=("parallel", "arbitrary"), vmem_limit_bytes=VMEM_LIMIT_BYTES),
        name="lru_mix",
    )(gate, xb, buf, h0.reshape(b, 1, LRU_W), *consts)
    return y, h_last.reshape(b, LRU_W)


def _prep_even(i, rwkv_mu, rwkv_w0, rwkv_w2, rwkv_a0, rwkv_a2, rwkv_g2, rwkv_k_k,
               rwkv_k_a, rwkv_r_k, rwkv_ln_w, rwkv_ln_b, gdn_conv_w, gdn_A_log, gdn_dt_bias, gdn_norm_w):
    n_ba = 2 * GDN_HEADS
    row = lambda x: x.reshape(1, -1)
    lane = jnp.arange(LANES) // RWKV_HD
    lora = DECAY_LORA + AAA_LORA
    rwkv = dict(
        mu=row(rwkv_mu[i]), w0=row(rwkv_w0[i]), a0=row(rwkv_a0[i]),
        w2=jnp.pad(rwkv_w2[i], ((0, lora - DECAY_LORA), (0, 0))).astype(BF16),
        a2=jnp.pad(rwkv_a2[i], ((DECAY_LORA, 0), (0, 0))).astype(BF16),
        g2=rwkv_g2[i].astype(BF16),
        k_k=row(rwkv_k_k[i]), k_a=row(rwkv_k_a[i]), r_k=row(rwkv_r_k[i]),
        ln_w=row(rwkv_ln_w[i]), ln_b=row(rwkv_ln_b[i]),
        hsum=(lane[:, None] == lane[None, :]).astype(BF16),
    )
    pad_heads = lambda x: jnp.pad(x, (GDN_HEADS, GDN_BA_PAD - n_ba)).reshape(1, GDN_BA_PAD)
    gdn = dict(conv_w=gdn_conv_w[i], a_log=pad_heads(gdn_A_log[i]), dt_bias=pad_heads(gdn_dt_bias[i]),
               norm_w=row(gdn_norm_w[i]))
    return rwkv, gdn


def _prep_odd(i, lru_conv_w, lru_conv_b, lru_wa, lru_ba, lru_wi, lru_bi, lru_L):
    lru = dict(
        conv_w=lru_conv_w[i], conv_b=lru_conv_b[i].reshape(1, LRU_W),
        wab=jnp.concatenate([lru_wa[i], lru_wi[i]], axis=-1).astype(BF16),
        bab=jnp.concatenate([lru_ba[i], lru_bi[i]], axis=-1).reshape(LRU_HEADS, 1, 2 * LRU_BW),
        l=lru_L[i].reshape(1, LRU_W),
    )
    return lru


def _prep_in_proj(w_in_even, w_in_odd):
    main = RWKV_PROJ + GDN_QKV + GDN_W
    n_ba = 2 * GDN_HEADS
    even_ba = jnp.pad(w_in_even[:, :, main:], ((0, 0), (0, 0), (0, GDN_BA_PAD - n_ba)))
    return w_in_even.astype(BF16), even_ba.astype(BF16), w_in_odd.astype(BF16)


def _row_tile(n):
    return 512 if n % 512 == 0 else 256


def _seq_group(b, c, rows):
    g = max(1, min(b, rows // c))
    while b % g:
        g -= 1
    return g


def _trunk(x, st_rwkv, st_shift, st_gdn, st_gconv, st_lru, st_lconv, in_w, out_w, mlp_w, even_p, odd_p,
           norm_mix, norm_mlp, norm_final):
    b, t, _ = x.shape
    n = b * t
    tm = _row_tile(n)
    c_even = min(64, t)
    gs_even = _seq_group(b, c_even, 128)
    n_stacks = max(k for k in (1, 2) if k * gs_even <= MAX_SEQS_PER_STEP and b % (k * gs_even) == 0)
    g_even = gs_even * n_stacks
    c_odd = min(256, t)
    g_odd = _seq_group(b, c_odd, 256)
    h = x.reshape(n, D_MODEL)
    n_shift, n_gconv, n_lru, n_lconv = [], [], [], []
    n_rwkv = n_gdn = None
    y_final = None
    for l in range(DEPTH):
        i = l // 2
        up, down = mlp_w
        wo = out_w[l % 2]
        final_nw = norm_final if l == DEPTH - 1 else None
        if l % 2 == 0:
            rwkv, gdn = even_p[i]
            p, qkv, z, ba = _norm_proj(h, norm_mix[l], in_w[:2], i,
                                       ((RWKV_PROJ, GDN_QKV, GDN_W), (GDN_BA_PAD,)), tm)
            p = p.reshape(b, t, RWKV_PROJ)
            qkv = qkv.reshape(b, t, GDN_QKV)
            ya, n_rwkv = _rwkv_mix(p, st_shift[i], st_rwkv, n_rwkv, i, rwkv, c_even, g_even, gs_even)
            yb, n_gdn = _gdn_mix(qkv, z.reshape(b, t, GDN_W), ba.reshape(b, t, GDN_BA_PAD), st_gconv[i],
                                 st_gdn, n_gdn, i, gdn, c_even, g_even, gs_even)
            n_shift.append(p[:, -1])
            n_gconv.append(qkv[:, t - (CONV_W - 1):])
            ys = (ya.reshape(n, RWKV_W), yb.reshape(n, GDN_W))
        else:
            lru = odd_p[i]
            gate, xb = _norm_proj(h, norm_mix[l], in_w[2:], i, ((LRU_W, LRU_W),), tm)
            xb = xb.reshape(b, t, LRU_W)
            yc, h_last = _lru_mix(gate.reshape(b, t, LRU_W), xb, st_lconv[i], st_lru[i], lru, c_odd, g_odd)
            n_lru.append(h_last)
            n_lconv.append(xb[:, t - (CONV_W - 1):])
            ys = (yc.reshape(n, LRU_W),)
        outs = _mix_out_mlp(h, ys, wo, i, norm_mlp[l], up, down, l, final_nw, tm)
        h = outs[0]
        if final_nw is not None:
            y_final = outs[1]
    return (y_final.reshape(b, t, D_MODEL), n_rwkv, jnp.stack(n_shift), n_gdn,
            jnp.stack(n_gconv), jnp.stack(n_lru), jnp.stack(n_lconv))


def kernel(x_prompt, x_sample, state_rwkv, state_rwkv_shift, state_gdn, state_gdn_conv, state_lru, state_lru_conv, norm_mix, norm_mlp, norm_final, w_in_even, w_out_even, rwkv_mu, rwkv_w0, rwkv_w2, rwkv_a0, rwkv_a2, rwkv_g2, rwkv_k_k, rwkv_k_a, rwkv_r_k, rwkv_ln_w, rwkv_ln_b, gdn_conv_w, gdn_A_log, gdn_dt_bias, gdn_norm_w, w_in_odd, w_out_odd, lru_conv_w, lru_conv_b, lru_wa, lru_ba, lru_wi, lru_bi, lru_L, mlp_up, mlp_down):
    n_even = (DEPTH + 1) // 2
    n_odd = DEPTH // 2
    even_p = [_prep_even(i, rwkv_mu, rwkv_w0, rwkv_w2, rwkv_a0, rwkv_a2, rwkv_g2,
                         rwkv_k_k, rwkv_k_a, rwkv_r_k, rwkv_ln_w, rwkv_ln_b, gdn_conv_w, gdn_A_log,
                         gdn_dt_bias, gdn_norm_w) for i in range(n_even)]
    odd_p = [_prep_odd(i, lru_conv_w, lru_conv_b, lru_wa, lru_ba, lru_wi, lru_bi, lru_L)
             for i in range(n_odd)]
    mlp_w = (mlp_up.astype(BF16), mlp_down.astype(BF16))
    out_w = (w_out_even.astype(BF16), w_out_odd.astype(BF16))
    bp = x_prompt.shape[0]
    dt = x_prompt.dtype
    zeros = lambda *s: jnp.zeros(s, dt)
    shared = (_prep_in_proj(w_in_even, w_in_odd), out_w, mlp_w, even_p, odd_p, norm_mix, norm_mlp, norm_final)
    y_p, rwkv_p, shift_p, gdn_p, gconv_p, lru_p, lconv_p = _trunk(
        x_prompt,
        zeros(n_even, bp, RWKV_HEADS, RWKV_HD, RWKV_HD), zeros(n_even, bp, RWKV_PROJ),
        zeros(n_even, bp, GDN_HEADS, GDN_HD, GDN_HD), zeros(n_even, bp, CONV_W - 1, GDN_QKV),
        zeros(n_odd, bp, LRU_W), zeros(n_odd, bp, CONV_W - 1, LRU_W), *shared)
    y_s, rwkv_s, shift_s, gdn_s, gconv_s, lru_s, lconv_s = _trunk(
        x_sample, state_rwkv, state_rwkv_shift, state_gdn, state_gdn_conv, state_lru, state_lru_conv, *shared)
    return (y_p, y_s, rwkv_p, rwkv_s, shift_p, shift_s, gdn_p, gdn_s, gconv_p, gconv_s, lru_p, lru_s,
            lconv_p, lconv_s)
```

```python
import functools

import jax
import jax.numpy as jnp
from jax import lax
from jax.experimental import pallas as pl
from jax.experimental.pallas import tpu as pltpu

F32 = jnp.float32
BF16 = jnp.bfloat16

D_MODEL = 1024
DEPTH = 4
CONV_W = 4
NORM_EPS = 1e-6
RWKV_HEADS = 8
RWKV_HD = 64
RWKV_W = RWKV_HEADS * RWKV_HD
RWKV_PAIRS = RWKV_HEADS // 2
DECAY_LORA = 64
AAA_LORA = 64
GATE_LORA = 128
RWKV_PROJ = 3 * RWKV_W + DECAY_LORA + AAA_LORA + GATE_LORA
RWKV_GN_EPS = 64e-5
RWKV_DECAY_SCALE = 0.6065306597126334
GDN_HEADS = 4
GDN_HD = 128
GDN_W = GDN_HEADS * GDN_HD
GDN_QKV = 3 * GDN_W
GDN_BA_PAD = 128
LRU_W = D_MODEL
LRU_HEADS = 8
LRU_BW = LRU_W // LRU_HEADS
LRU_C = 8.0
D_FF = 4 * D_MODEL

SUBLANES = 8
LANES = 128
VMEM_LIMIT_BYTES = 56 * 1024 * 1024
MAX_SEQS_PER_STEP = 16


def _mm(a, b):
    return jnp.dot(a.astype(BF16), b.astype(BF16), preferred_element_type=F32)


def _mm_nt(a, b):
    return lax.dot_general(a.astype(BF16), b.astype(BF16), (((1,), (1,)), ((), ())),
                           preferred_element_type=F32)


def _mm_tn(a, b):
    return lax.dot_general(a.astype(BF16), b.astype(BF16), (((0,), (0,)), ((), ())),
                           preferred_element_type=F32)


def _split2(x):
    hi = x.astype(BF16)
    return hi, (x - hi.astype(F32)).astype(BF16)


def _mm_exact_lhs(m01, x):
    hi, lo = _split2(x)
    d = lambda t: jnp.dot(m01, t, preferred_element_type=F32)
    return d(hi) + d(lo)


def _mm_exact_rhs(x, m01):
    hi, lo = _split2(x)
    d = lambda t: jnp.dot(t, m01, preferred_element_type=F32)
    return d(hi) + d(lo)


def _mm3(a, b):
    ah, al = _split2(a)
    bh, bl = _split2(b)
    d = lambda x, y: jnp.dot(x, y, preferred_element_type=F32)
    return d(ah, bh) + d(ah, bl) + d(al, bh)


def _softplus(x):
    return jnp.maximum(x, 0.0) + jnp.log1p(jnp.exp(-jnp.abs(x)))


def _sigmoid(x):
    return jax.nn.sigmoid(x)


def _silu(x):
    return x * jax.nn.sigmoid(x)


def _gelu_tanh(x):
    return 0.5 * x * (1.0 + jnp.tanh(0.7978845608028654 * (x + 0.044715 * (x * x * x))))


def _block_masks(n, c):
    row = lax.broadcasted_iota(jnp.int32, (n, n), 0)
    col = lax.broadcasted_iota(jnp.int32, (n, n), 1)
    same = jnp.bitwise_xor(row, col) < c
    return same & (row >= col), same & (row > col), same, row == col


def _unit_lower_inverses(ms, eye, c):
    n = eye.shape[0]
    row = lax.broadcasted_iota(jnp.int32, (n, n), 0)
    col = lax.broadcasted_iota(jnp.int32, (n, n), 1)
    corner = lambda s: ((jnp.bitwise_xor(row, col) < 2 * s) & (jnp.bitwise_and(row, s) != 0)
                        & (jnp.bitwise_and(col, s) == 0))
    ps = [eye + jnp.where(corner(1), m, 0.0) for m in ms]
    s = 2
    while s < c:
        mask = corner(s)
        pb = [_mm(p, jnp.where(mask, m, 0.0)) for p, m in zip(ps, ms)]
        ps = [p + _mm(x, p) for p, x in zip(ps, pb)]
        s *= 2
    return ps


def _rows(x, s, c):
    return x[s * c:(s + 1) * c]


def _cat(xs, axis=0):
    return xs[0] if len(xs) == 1 else jnp.concatenate(xs, axis=axis)


def _const_spec(shape):
    nd = len(shape)
    return pl.BlockSpec(shape, lambda *_: (0,) * nd)


def _skip_first_ref(kern, _, *refs):
    return kern(*refs)


def _in_place_slot(kern, s_prev, out_index):
    if s_prev is None:
        return kern, [], [], {}
    return (functools.partial(_skip_first_ref, kern), [pl.BlockSpec(memory_space=pl.ANY)], [s_prev],
            {0: out_index})


def _causal_conv(x_ref, xp_ref, cw_ref, s, c):
    base = SUBLANES
    nb = CONV_W - 1
    xp_ref[s, base:base + c, :] = x_ref[s]
    y = cw_ref[0:1, :] * xp_ref[s, base - nb:base - nb + c, :]
    for j in range(1, CONV_W):
        y = y + cw_ref[j:j + 1, :] * xp_ref[s, base - nb + j:base - nb + j + c, :]
    xp_ref[s, base - nb:base, :] = xp_ref[s, base + c - nb:base + c, :]
    return y


def _norm_proj_kernel(x_ref, nw_ref, *refs, groups, col_chunk):
    n_w = len(groups)
    w_refs, o_refs = refs[:n_w], refs[n_w:]
    x = x_ref[...]
    var = jnp.mean(x * x, axis=-1, keepdims=True)
    xn = (x * lax.rsqrt(var + NORM_EPS) * nw_ref[...]).astype(BF16)
    o_iter = iter(o_refs)
    for w_ref, widths in zip(w_refs, groups):
        off = 0
        for width in widths:
            o_ref = next(o_iter)
            for c0 in range(0, width, col_chunk):
                cw = min(col_chunk, width - c0)
                o_ref[:, c0:c0 + cw] = jnp.dot(xn, w_ref[:, off + c0:off + c0 + cw],
                                               preferred_element_type=F32)
            off += width


def _norm_proj(x, nw, ws, layer, groups, tm):
    n = x.shape[0]
    assert n % tm == 0 and all(sum(gr) <= w.shape[2] for gr, w in zip(groups, ws))
    kern = functools.partial(_norm_proj_kernel, groups=tuple(tuple(gr) for gr in groups), col_chunk=512)
    widths = [wd for gr in groups for wd in gr]
    return pl.pallas_call(
        kern,
        grid=(n // tm,),
        in_specs=[
            pl.BlockSpec((tm, D_MODEL), lambda i: (i, 0)),
            _const_spec((1, D_MODEL)),
        ] + [pl.BlockSpec((None, D_MODEL, w.shape[2]), lambda i: (layer, 0, 0)) for w in ws],
        out_specs=[pl.BlockSpec((tm, wd), lambda i: (i, 0)) for wd in widths],
        out_shape=[jax.ShapeDtypeStruct((n, wd), F32) for wd in widths],
        compiler_params=pltpu.CompilerParams(
            dimension_semantics=("parallel",), vmem_limit_bytes=VMEM_LIMIT_BYTES),
        name="norm_proj",
    )(x, nw.reshape(1, D_MODEL), *ws)


def _mlp_kernel(*refs, n_mix, ff_chunk, final):
    h_ref = refs[0]
    y_refs = refs[1:1 + n_mix]
    wo_ref, nw_ref, up_ref, down_ref = refs[1 + n_mix:5 + n_mix]
    rest = refs[5 + n_mix:]
    if final:
        fnw_ref, o_ref, yf_ref = rest
    else:
        (o_ref,) = rest
    y = _cat([y_ref[...].astype(BF16) for y_ref in y_refs], axis=1)
    h = h_ref[...] + jnp.dot(y, wo_ref[...], preferred_element_type=F32)
    var = jnp.mean(h * h, axis=-1, keepdims=True)
    xn = (h * lax.rsqrt(var + NORM_EPS) * nw_ref[...]).astype(BF16)
    acc = h
    for c0 in range(0, D_FF, ff_chunk):
        u = jnp.dot(xn, up_ref[:, c0:c0 + ff_chunk], preferred_element_type=F32)
        r = jnp.square(jnp.maximum(u, 0.0)).astype(BF16)
        acc = acc + jnp.dot(r, down_ref[c0:c0 + ff_chunk, :], preferred_element_type=F32)
    o_ref[...] = acc
    if final:
        var2 = jnp.mean(acc * acc, axis=-1, keepdims=True)
        yf_ref[...] = acc * lax.rsqrt(var2 + NORM_EPS) * fnw_ref[...]


def _mix_out_mlp(h, ys, wo, wo_layer, nw, up, down, layer, final_nw, tm):
    n = h.shape[0]
    n_mix = len(ys)
    final = final_nw is not None
    kern = functools.partial(_mlp_kernel, n_mix=n_mix, ff_chunk=512, final=final)
    row = lambda wd: pl.BlockSpec((tm, wd), lambda i: (i, 0))
    pick = lambda w, l: pl.BlockSpec((None,) + w.shape[1:], lambda i: (l, 0, 0))
    in_specs = [row(D_MODEL)] + [row(y.shape[1]) for y in ys]
    in_specs += [pick(wo, wo_layer), _const_spec((1, D_MODEL)), pick(up, layer), pick(down, layer)]
    args = [h, *ys, wo, nw.reshape(1, D_MODEL), up, down]
    out_specs = [row(D_MODEL)]
    out_shape = [jax.ShapeDtypeStruct((n, D_MODEL), F32)]
    if final:
        in_specs.append(_const_spec((1, D_MODEL)))
        args.append(final_nw.reshape(1, D_MODEL))
        out_specs.append(row(D_MODEL))
        out_shape.append(jax.ShapeDtypeStruct((n, D_MODEL), F32))
    return pl.pallas_call(
        kern,
        grid=(n // tm,),
        in_specs=in_specs,
        out_specs=out_specs,
        out_shape=out_shape,
        compiler_params=pltpu.CompilerParams(
            dimension_semantics=("parallel",), vmem_limit_bytes=VMEM_LIMIT_BYTES),
        name="mix_out_mlp",
    )(*args)


def _rwkv_kernel(p_ref, prev_ref, s0_ref, mu_ref, w0_ref, w2_ref, a0_ref, a2_ref, g2_ref,
                 kk_ref, ka_ref, rk_ref, lnw_ref, lnb_ref, hsum_ref,
                 out_ref, s_out_ref, xp_ref, state_ref, *, c, g, gs, nc):
    ci = pl.program_id(1)
    base = SUBLANES
    gc = g * c

    hd = RWKV_HD

    @pl.when(ci == 0)
    def _():
        for s in range(g):
            xp_ref[s, base - 1:base, :] = prev_ref[s]
        state_ref[...] = jnp.zeros(state_ref.shape, F32)
        for s in range(g):
            for j in range(RWKV_PAIRS):
                state_ref[s, j, 0:hd, 0:hd] = s0_ref[s, 2 * j]
                state_ref[s, j, hd:2 * hd, hd:2 * hd] = s0_ref[s, 2 * j + 1]

    shifted = []
    for s in range(g):
        ps = p_ref[s]
        xp_ref[s, base:base + c, :] = ps
        shifted.append(xp_ref[s, base - 1:base - 1 + c, :])
        xp_ref[s, base - 1:base, :] = ps[c - 1:c, :]
    p = p_ref[...].reshape(gc, RWKV_PROJ)
    shifted = _cat(shifted)

    xs = p + mu_ref[...] * (shifted - p)
    r = xs[:, 0:RWKV_W]
    k = xs[:, RWKV_W:2 * RWKV_W]
    v = xs[:, 2 * RWKV_W:3 * RWKV_W]
    xwa = xs[:, 3 * RWKV_W:3 * RWKV_W + DECAY_LORA + AAA_LORA]
    xg = xs[:, 3 * RWKV_W + DECAY_LORA + AAA_LORA:]

    ld = -RWKV_DECAY_SCALE * _sigmoid(w0_ref[...] + _mm(jnp.tanh(xwa), w2_ref[...]))
    a = _sigmoid(a0_ref[...] + _mm(xwa, a2_ref[...]))
    gate = _mm(_sigmoid(xg), g2_ref[...])

    incl1, _, same1, _ = _block_masks(gc, c)
    sums = _mm_exact_lhs(jnp.concatenate([incl1, same1], axis=0).astype(BF16), ld)
    cum = sums[:gc]
    tot = sums[gc:]
    w_in = jnp.exp(cum)
    w_ex = jnp.exp(cum - ld)
    w_inv = jnp.exp(-cum)
    w_tail = jnp.exp(tot - cum)
    w_last = jnp.exp(tot)

    hsum = hsum_ref[...]

    def head_sums(x):
        return jnp.concatenate(
            [_mm_exact_rhs(x[:, j * LANES:(j + 1) * LANES], hsum) for j in range(RWKV_PAIRS)], axis=1)

    kkr = k * kk_ref[...]
    kk = kkr * lax.rsqrt(head_sums(kkr * kkr) + 1e-6)
    kp = k * (1.0 + (a - 1.0) * ka_ref[...])
    bv = kk * a

    ns = g // gs
    n = gs * c
    nh = RWKV_HEADS
    lo = lax.broadcasted_iota(jnp.int32, (n, LANES), 1) < RWKV_HD

    def units(x):
        out = []
        for st in range(ns):
            for j in range(RWKV_PAIRS):
                slab = x[st * n:(st + 1) * n, j * LANES:(j + 1) * LANES]
                out += [jnp.where(lo, slab, 0.0), jnp.where(lo, 0.0, slab)]
        return out

    a_t = units(-kk * w_ex)
    r_t = units(r * w_in)
    b_t = units(bv * w_inv)
    k_t = units(kp * w_inv)
    v_b = units(v)
    b_w = units(bv * w_tail)
    k_w = units(kp * w_tail)

    us_ = range(ns * nh)
    incl, strict, _, diag = _block_masks(n, c)
    eye = diag.astype(F32)
    ars = [jnp.concatenate([a_t[u], r_t[u]], axis=0) for u in us_]
    gbs = [_mm_nt(ars[u], b_t[u]) for u in us_]
    gks = [_mm_nt(ars[u], k_t[u]) for u in us_]
    m_ab = [jnp.where(strict, x[:n], 0.0) for x in gbs]
    m_ak = [jnp.where(strict, x[:n], 0.0) for x in gks]
    l_rb = [jnp.where(incl, x[n:], 0.0) for x in gbs]
    l_rk = [jnp.where(incl, x[n:], 0.0) for x in gks]
    t_inv = _unit_lower_inverses(m_ab, eye, c)

    sa = [[] for _ in us_]
    sr = [[] for _ in us_]
    for st in range(ns):
        for j in range(RWKV_PAIRS):
            u0 = st * nh + 2 * j
            for sl in range(gs):
                lhs = jnp.concatenate([_rows(x[u0 + i], sl, c) for x in (a_t, r_t) for i in range(2)], axis=0)
                ps = _mm_nt(lhs, state_ref[st * gs + sl, j])
                for i in range(2):
                    sa[u0 + i].append(_rows(ps, i, c))
                    sr[u0 + i].append(_rows(ps, 2 + i, c))
    sa = [_cat(x) for x in sa]
    sr = [_cat(x) for x in sr]

    mv = [_mm(m_ak[u], v_b[u]) for u in us_]
    us = [_mm3(t_inv[u], sa[u] + mv[u]) for u in us_]
    ys = [sr[u] + _mm(l_rb[u], us[u]) + _mm(l_rk[u], v_b[u]) for u in us_]
    y = _cat([jnp.concatenate([ys[st * nh + 2 * j] + ys[st * nh + 2 * j + 1] for j in range(RWKV_PAIRS)], axis=1)
              for st in range(ns)])

    for st in range(ns):
        for j in range(RWKV_PAIRS):
            u0 = st * nh + 2 * j
            for sl in range(gs):
                s = st * gs + sl
                uv = jnp.concatenate([_rows(x[u0 + i], sl, c) for x in (us, v_b) for i in range(2)], axis=0)
                bk = jnp.concatenate([_rows(x[u0 + i], sl, c) for x in (b_w, k_w) for i in range(2)], axis=0)
                decay = w_last[s * c:s * c + 1, j * LANES:(j + 1) * LANES]
                state_ref[s, j] = state_ref[s, j] * decay + _mm_tn(uv, bk)

    inv_hd = 1.0 / RWKV_HD
    sums = head_sums(jnp.concatenate([y, y * y, r * kp * rk_ref[...]], axis=0))
    mean = sums[:gc] * inv_hd
    var = sums[gc:2 * gc] * inv_hd - mean * mean
    yn = (y - mean) * lax.rsqrt(var + RWKV_GN_EPS) * lnw_ref[...] + lnb_ref[...]
    out_ref[...] = ((yn + sums[2 * gc:] * v) * gate).reshape(g, c, RWKV_W)

    @pl.when(ci == nc - 1)
    def _():
        for s in range(g):
            for j in range(RWKV_PAIRS):
                s_out_ref[s, 2 * j] = state_ref[s, j, 0:hd, 0:hd]
                s_out_ref[s, 2 * j + 1] = state_ref[s, j, hd:2 * hd, hd:2 * hd]


def _rwkv_mix(p, prev, s0, s_prev, layer, prm, c, g, gs):
    b, t, _ = p.shape
    nc = t // c
    assert nc * c == t and c % SUBLANES == 0 and b % g == 0
    kern = functools.partial(_rwkv_kernel, c=c, g=g, gs=gs, nc=nc)
    kern, prev_specs, prev_args, aliases = _in_place_slot(kern, s_prev, out_index=1)
    consts = [prm["mu"], prm["w0"], prm["w2"], prm["a0"], prm["a2"], prm["g2"], prm["k_k"], prm["k_a"],
              prm["r_k"], prm["ln_w"], prm["ln_b"], prm["hsum"]]
    st_block = (g, RWKV_HEADS, RWKV_HD, RWKV_HD)
    return pl.pallas_call(
        kern,
        grid=(b // g, nc),
        in_specs=prev_specs + [
            pl.BlockSpec((g, c, RWKV_PROJ), lambda i, j: (i, j, 0)),
            pl.BlockSpec((g, 1, RWKV_PROJ), lambda i, j: (i, 0, 0)),
            pl.BlockSpec((None,) + st_block, lambda i, j: (layer, i, 0, 0, 0)),
        ] + [_const_spec(x.shape) for x in consts],
        out_specs=[
            pl.BlockSpec((g, c, RWKV_W), lambda i, j: (i, j, 0)),
            pl.BlockSpec((None,) + st_block, lambda i, j: (layer, i, 0, 0, 0)),
        ],
        out_shape=[
            jax.ShapeDtypeStruct((b, t, RWKV_W), F32),
            jax.ShapeDtypeStruct(s0.shape, F32),
        ],
        input_output_aliases=aliases,
        scratch_shapes=[
            pltpu.VMEM((g, c + SUBLANES, RWKV_PROJ), F32),
            pltpu.VMEM((g, RWKV_PAIRS, LANES, LANES), F32),
        ],
        compiler_params=pltpu.CompilerParams(
            dimension_semantics=("parallel", "arbitrary"), vmem_limit_bytes=VMEM_LIMIT_BYTES),
        name="rwkv7_mix",
    )(*prev_args, p, prev.reshape(b, 1, RWKV_PROJ), s0, *consts)


def _gdn_kernel(qkv_ref, z_ref, ba_ref, buf_ref, s0_ref, cw_ref, alog_ref, dt_ref, nw_ref,
                out_ref, s_out_ref, xp_ref, state_ref, *, c, g, gs, nc):
    ci = pl.program_id(1)
    base = SUBLANES
    nb = CONV_W - 1
    gc = g * c

    @pl.when(ci == 0)
    def _():
        for s in range(g):
            xp_ref[s, base - nb:base, :] = buf_ref[s]
        state_ref[...] = s0_ref[...]

    qkv = _silu(_cat([_causal_conv(qkv_ref, xp_ref, cw_ref, s, c) for s in range(g)]))
    z = z_ref[...].reshape(gc, GDN_W)
    ba = ba_ref[...].reshape(gc, GDN_BA_PAD)
    beta_all = _sigmoid(ba)
    g_all = -jnp.exp(alog_ref[...]) * _softplus(ba + dt_ref[...])
    incl_all, strict_all, same_all, _ = _block_masks(gc, c)
    sums = _mm_exact_lhs(jnp.concatenate([incl_all, same_all], axis=0).astype(BF16), g_all)
    gc_all = sums[:gc]
    tot_all = sums[gc:]
    upper = (same_all & jnp.logical_not(strict_all)).astype(BF16)
    dn = (((0,), (0,)), ((), ()))
    gct = sum(lax.dot_general(t, upper, dn, preferred_element_type=F32) for t in _split2(g_all))

    ns = g // gs
    n = gs * c
    incl, strict, _, diag = _block_masks(n, c)
    eye = diag.astype(F32)
    stack = lambda x, st: x[st * n:(st + 1) * n]

    hs = range(GDN_HEADS)
    head = lambda x, part, h: x[:, part * GDN_W + h * GDN_HD:part * GDN_W + (h + 1) * GDN_HD]
    q = [head(qkv, 0, h) for h in hs]
    k = [head(qkv, 1, h) for h in hs]
    v = [head(qkv, 2, h) for h in hs]
    q = [x * lax.rsqrt(jnp.sum(x * x, axis=-1, keepdims=True) + 1e-6) * (GDN_HD ** -0.5) for x in q]
    k = [x * lax.rsqrt(jnp.sum(x * x, axis=-1, keepdims=True) + 1e-6) for x in k]
    beta = [beta_all[:, h:h + 1] for h in hs]
    gcol = [gc_all[:, GDN_HEADS + h:GDN_HEADS + h + 1] for h in hs]
    grow = [gct[GDN_HEADS + h:GDN_HEADS + h + 1, :] for h in hs]
    glast = [tot_all[:, GDN_HEADS + h:GDN_HEADS + h + 1] for h in hs]
    kb = [k[h] * beta[h] for h in hs]
    vb = [v[h] * beta[h] for h in hs]
    eg = [jnp.exp(gcol[h]) for h in hs]
    qe = [q[h] * eg[h] for h in hs]
    kbe = [kb[h] * eg[h] for h in hs]
    k_tail = [k[h] * jnp.exp(glast[h] - gcol[h]) for h in hs]

    units = [(st, h) for st in range(ns) for h in hs]
    decay = [jnp.where(incl, jnp.exp(jnp.where(incl, stack(gcol[h], st) - grow[h][:, st * n:(st + 1) * n], 0.0)),
                       0.0) for st, h in units]
    gq = [_mm_nt(jnp.concatenate([stack(q[h], st), stack(kb[h], st)], axis=0), stack(k[h], st))
          for st, h in units]
    qk = [jnp.where(incl, x[:n] * d, 0.0) for x, d in zip(gq, decay)]
    lmat = [jnp.where(strict, x[n:] * d, 0.0) for x, d in zip(gq, decay)]
    t_inv = _unit_lower_inverses([-x for x in lmat], eye, c)
    uw = [_mm3(t, jnp.concatenate([stack(vb[h], st), stack(kbe[h], st)], axis=1))
          for t, (st, h) in zip(t_inv, units)]

    q_s, w_s = [], []
    for u, (st, h) in enumerate(units):
        pieces = []
        for sl in range(gs):
            lhs = jnp.concatenate([_rows(stack(qe[h], st), sl, c), _rows(uw[u][:, GDN_HD:], sl, c)], axis=0)
            pieces.append(_mm(lhs, state_ref[st * gs + sl, h]))
        q_s.append(_cat([x[:c] for x in pieces]))
        w_s.append(_cat([x[c:] for x in pieces]))
    v_new = [uw[u][:, :GDN_HD] - w_s[u] for u in range(len(units))]
    o_u = [q_s[u] + _mm(qk[u], v_new[u]) for u in range(len(units))]
    for u, (st, h) in enumerate(units):
        for sl in range(gs):
            s = st * gs + sl
            decay_s = jnp.exp(glast[h][s * c:s * c + 1, :])
            state_ref[s, h] = (state_ref[s, h] * decay_s
                               + _mm_tn(_rows(stack(k_tail[h], st), sl, c), _rows(v_new[u], sl, c)))
    o = [_cat([o_u[st * GDN_HEADS + h] for st in range(ns)]) for h in hs]
    outs = []
    for h in hs:
        on = o[h] * lax.rsqrt(jnp.mean(o[h] * o[h], axis=-1, keepdims=True) + NORM_EPS) * nw_ref[...]
        outs.append(on * _silu(z[:, h * GDN_HD:(h + 1) * GDN_HD]))
    out_ref[...] = jnp.concatenate(outs, axis=1).reshape(g, c, GDN_W)

    @pl.when(ci == nc - 1)
    def _():
        s_out_ref[...] = state_ref[...]


def _gdn_mix(qkv, z, ba, buf, s0, s_prev, layer, prm, c, g, gs):
    b, t, _ = qkv.shape
    nc = t // c
    assert nc * c == t and c % SUBLANES == 0 and b % g == 0
    kern = functools.partial(_gdn_kernel, c=c, g=g, gs=gs, nc=nc)
    kern, prev_specs, prev_args, aliases = _in_place_slot(kern, s_prev, out_index=1)
    consts = [prm["conv_w"], prm["a_log"], prm["dt_bias"], prm["norm_w"]]
    st_block = (g, GDN_HEADS, GDN_HD, GDN_HD)
    return pl.pallas_call(
        kern,
        grid=(b // g, nc),
        in_specs=prev_specs + [
            pl.BlockSpec((g, c, GDN_QKV), lambda i, j: (i, j, 0)),
            pl.BlockSpec((g, c, GDN_W), lambda i, j: (i, j, 0)),
            pl.BlockSpec((g, c, GDN_BA_PAD), lambda i, j: (i, j, 0)),
            pl.BlockSpec((g, CONV_W - 1, GDN_QKV), lambda i, j: (i, 0, 0)),
            pl.BlockSpec((None,) + st_block, lambda i, j: (layer, i, 0, 0, 0)),
        ] + [_const_spec(x.shape) for x in consts],
        out_specs=[
            pl.BlockSpec((g, c, GDN_W), lambda i, j: (i, j, 0)),
            pl.BlockSpec((None,) + st_block, lambda i, j: (layer, i, 0, 0, 0)),
        ],
        out_shape=[
            jax.ShapeDtypeStruct((b, t, GDN_W), F32),
            jax.ShapeDtypeStruct(s0.shape, F32),
        ],
        input_output_aliases=aliases,
        scratch_shapes=[
            pltpu.VMEM((g, c + SUBLANES, GDN_QKV), F32),
            pltpu.VMEM(st_block, F32),
        ],
        compiler_params=pltpu.CompilerParams(
            dimension_semantics=("parallel", "arbitrary"), vmem_limit_bytes=VMEM_LIMIT_BYTES),
        name="gdn_mix",
    )(*prev_args, qkv, z, ba, buf, s0, *consts)


def _lru_kernel(gate_ref, xb_ref, buf_ref, h0_ref, cw_ref, cb_ref, wab_ref, bab_ref, l_ref,
                out_ref, h_out_ref, xp_ref, carry_ref, sa_ref, sb_ref, *, c, g, nc):
    ci = pl.program_id(1)
    base = SUBLANES
    nb = CONV_W - 1
    gc = g * c

    @pl.when(ci == 0)
    def _():
        for s in range(g):
            xp_ref[s, base - nb:base, :] = buf_ref[s]
        carry_ref[...] = h0_ref[...]

    xc = _cat([_causal_conv(xb_ref, xp_ref, cw_ref, s, c) for s in range(g)]) + cb_ref[...]

    for h in range(LRU_HEADS):
        sl = slice(h * LRU_BW, (h + 1) * LRU_BW)
        xh = xc[:, sl]
        ri = _mm(xh, wab_ref[h]) + bab_ref[h]
        r = _sigmoid(ri[:, :LRU_BW])
        i = _sigmoid(ri[:, LRU_BW:])
        log_a = -LRU_C * r * _softplus(-l_ref[:, sl])
        a = jnp.exp(log_a)
        mult = jnp.sqrt(-jnp.tanh(log_a) * (a * a + 1.0))
        sa_ref[:, sl] = a
        sb_ref[:, sl] = mult * i * xh

    tpos = jnp.bitwise_and(lax.broadcasted_iota(jnp.int32, (gc, 1), 0), SUBLANES - 1)
    a = sa_ref[...]
    b = sb_ref[...]

    def shift_in_group(x, d):
        x3 = x.reshape(gc // SUBLANES, SUBLANES, LRU_W)
        return pltpu.roll(x3, d, axis=1).reshape(gc, LRU_W)

    d = 1
    while d < SUBLANES:
        live = tpos >= d
        a_sh = jnp.where(live, shift_in_group(a, d), 1.0)
        b_sh = jnp.where(live, shift_in_group(b, d), 0.0)
        b = a * b_sh + b
        a = a * a_sh
        d *= 2
    groups = []
    for s in range(g):
        h = carry_ref[s]
        for r0 in range(s * c, (s + 1) * c, SUBLANES):
            hg = a[r0:r0 + SUBLANES] * h + b[r0:r0 + SUBLANES]
            groups.append(hg)
            h = hg[SUBLANES - 1:SUBLANES]
        carry_ref[s] = h
    hs = _cat(groups)
    out_ref[...] = (hs * _gelu_tanh(gate_ref[...].reshape(gc, LRU_W))).reshape(g, c, LRU_W)

    @pl.when(ci == nc - 1)
    def _():
        h_out_ref[...] = carry_ref[...]


def _lru_mix(gate, xb, buf, h0, prm, c, g):
    b, t, _ = xb.shape
    nc = t // c
    assert nc * c == t and c % SUBLANES == 0 and b % g == 0
    kern = functools.partial(_lru_kernel, c=c, g=g, nc=nc)
    consts = [prm["conv_w"], prm["conv_b"], prm["wab"], prm["bab"], prm["l"]]
    seq = lambda rows: pl.BlockSpec((g, rows, LRU_W), lambda i, j: (i, 0, 0))
    chunk = pl.BlockSpec((g, c, LRU_W), lambda i, j: (i, j, 0))
    y, h_last = pl.pallas_call(
        kern,
        grid=(b // g, nc),
        in_specs=[chunk, chunk, seq(CONV_W - 1), seq(1)] + [_const_spec(x.shape) for x in consts],
        out_specs=[chunk, seq(1)],
        out_shape=[
            jax.ShapeDtypeStruct((b, t, LRU_W), F32),
            jax.ShapeDtypeStruct((b, 1, LRU_W), F32),
        ],
        scratch_shapes=[
            pltpu.VMEM((g, c + SUBLANES, LRU_W), F32),
            pltpu.VMEM((g, 1, LRU_W), F32),
            pltpu.VMEM((g * c, LRU_W), F32),
            pltpu.VMEM((g * c, LRU_W), F32),
        ],
        compiler_params=pltpu.CompilerParams(
            dimension_semantics=("parallel", "arbitrary"), vmem_limit_bytes=VMEM_LIMIT_BYTES),
        name="lru_mix",
    )(gate, xb, buf, h0.reshape(b, 1, LRU_W), *consts)
    return y, h_last.reshape(b, LRU_W)


def _prep_even(i, rwkv_mu, rwkv_w0, rwkv_w2, rwkv_a0, rwkv_a2, rwkv_g2, rwkv_k_k,
               rwkv_k_a, rwkv_r_k, rwkv_ln_w, rwkv_ln_b, gdn_conv_w, gdn_A_log, gdn_dt_bias, gdn_norm_w):
    n_ba = 2 * GDN_HEADS
    row = lambda x: x.reshape(1, -1)
    lane = jnp.arange(LANES) // RWKV_HD
    lora = DECAY_LORA + AAA_LORA
    rwkv = dict(
        mu=row(rwkv_mu[i]), w0=row(rwkv_w0[i]), a0=row(rwkv_a0[i]),
        w2=jnp.pad(rwkv_w2[i], ((0, lora - DECAY_LORA), (0, 0))).astype(BF16),
        a2=jnp.pad(rwkv_a2[i], ((DECAY_LORA, 0), (0, 0))).astype(BF16),
        g2=rwkv_g2[i].astype(BF16),
        k_k=row(rwkv_k_k[i]), k_a=row(rwkv_k_a[i]), r_k=row(rwkv_r_k[i]),
        ln_w=row(rwkv_ln_w[i]), ln_b=row(rwkv_ln_b[i]),
        hsum=(lane[:, None] == lane[None, :]).astype(BF16),
    )
    pad_heads = lambda x: jnp.pad(x, (GDN_HEADS, GDN_BA_PAD - n_ba)).reshape(1, GDN_BA_PAD)
    gdn = dict(conv_w=gdn_conv_w[i], a_log=pad_heads(gdn_A_log[i]), dt_bias=pad_heads(gdn_dt_bias[i]),
               norm_w=row(gdn_norm_w[i]))
    return rwkv, gdn


def _prep_odd(i, lru_conv_w, lru_conv_b, lru_wa, lru_ba, lru_wi, lru_bi, lru_L):
    lru = dict(
        conv_w=lru_conv_w[i], conv_b=lru_conv_b[i].reshape(1, LRU_W),
        wab=jnp.concatenate([lru_wa[i], lru_wi[i]], axis=-1).astype(BF16),
        bab=jnp.concatenate([lru_ba[i], lru_bi[i]], axis=-1).reshape(LRU_HEADS, 1, 2 * LRU_BW),
        l=lru_L[i].reshape(1, LRU_W),
    )
    return lru


def _prep_in_proj(w_in_even, w_in_odd):
    main = RWKV_PROJ + GDN_QKV + GDN_W
    n_ba = 2 * GDN_HEADS
    even_ba = jnp.pad(w_in_even[:, :, main:], ((0, 0), (0, 0), (0, GDN_BA_PAD - n_ba)))
    return w_in_even.astype(BF16), even_ba.astype(BF16), w_in_odd.astype(BF16)


def _row_tile(n):
    return 512 if n % 512 == 0 else 256


def _seq_group(b, c, rows):
    g = max(1, min(b, rows // c))
    while b % g:
        g -= 1
    return g


def _trunk(x, st_rwkv, st_shift, st_gdn, st_gconv, st_lru, st_lconv, in_w, out_w, mlp_w, even_p, odd_p,
           norm_mix, norm_mlp, norm_final):
    b, t, _ = x.shape
    n = b * t
    tm = _row_tile(n)
    c_even = min(64, t)
    gs_even = _seq_group(b, c_even, 128)
    fits = lambda k: k * gs_even <= MAX_SEQS_PER_STEP and b % (k * gs_even) == 0
    g_rwkv = gs_even * max(k for k in (1, 2) if fits(k))
    g_gdn = gs_even * max(k for k in (1, 2, 4) if fits(k))
    c_odd = min(256, t)
    g_odd = _seq_group(b, c_odd, 256)
    h = x.reshape(n, D_MODEL)
    n_shift, n_gconv, n_lru, n_lconv = [], [], [], []
    n_rwkv = n_gdn = None
    y_final = None
    for l in range(DEPTH):
        i = l // 2
        up, down = mlp_w
        wo = out_w[l % 2]
        final_nw = norm_final if l == DEPTH - 1 else None
        if l % 2 == 0:
            rwkv, gdn = even_p[i]
            p, qkv, z, ba = _norm_proj(h, norm_mix[l], in_w[:2], i,
                                       ((RWKV_PROJ, GDN_QKV, GDN_W), (GDN_BA_PAD,)), tm)
            p = p.reshape(b, t, RWKV_PROJ)
            qkv = qkv.reshape(b, t, GDN_QKV)
            ya, n_rwkv = _rwkv_mix(p, st_shift[i], st_rwkv, n_rwkv, i, rwkv, c_even, g_rwkv, gs_even)
            yb, n_gdn = _gdn_mix(qkv, z.reshape(b, t, GDN_W), ba.reshape(b, t, GDN_BA_PAD), st_gconv[i],
                                 st_gdn, n_gdn, i, gdn, c_even, g_gdn, gs_even)
            n_shift.append(p[:, -1])
            n_gconv.append(qkv[:, t - (CONV_W - 1):])
            ys = (ya.reshape(n, RWKV_W), yb.reshape(n, GDN_W))
        else:
            lru = odd_p[i]
            gate, xb = _norm_proj(h, norm_mix[l], in_w[2:], i, ((LRU_W, LRU_W),), tm)
            xb = xb.reshape(b, t, LRU_W)
            yc, h_last = _lru_mix(gate.reshape(b, t, LRU_W), xb, st_lconv[i], st_lru[i], lru, c_odd, g_odd)
            n_lru.append(h_last)
            n_lconv.append(xb[:, t - (CONV_W - 1):])
            ys = (yc.reshape(n, LRU_W),)
        outs = _mix_out_mlp(h, ys, wo, i, norm_mlp[l], up, down, l, final_nw, tm)
        h = outs[0]
        if final_nw is not None:
            y_final = outs[1]
    return (y_final.reshape(b, t, D_MODEL), n_rwkv, jnp.stack(n_shift), n_gdn,
            jnp.stack(n_gconv), jnp.stack(n_lru), jnp.stack(n_lconv))


def kernel(x_prompt, x_sample, state_rwkv, state_rwkv_shift, state_gdn, state_gdn_conv, state_lru, state_lru_conv, norm_mix, norm_mlp, norm_final, w_in_even, w_out_even, rwkv_mu, rwkv_w0, rwkv_w2, rwkv_a0, rwkv_a2, rwkv_g2, rwkv_k_k, rwkv_k_a, rwkv_r_k, rwkv_ln_w, rwkv_ln_b, gdn_conv_w, gdn_A_log, gdn_dt_bias, gdn_norm_w, w_in_odd, w_out_odd, lru_conv_w, lru_conv_b, lru_wa, lru_ba, lru_wi, lru_bi, lru_L, mlp_up, mlp_down):
    n_even = (DEPTH + 1) // 2
    n_odd = DEPTH // 2
    even_p = [_prep_even(i, rwkv_mu, rwkv_w0, rwkv_w2, rwkv_a0, rwkv_a2, rwkv_g2,
                         rwkv_k_k, rwkv_k_a, rwkv_r_k, rwkv_ln_w, rwkv_ln_b, gdn_conv_w, gdn_A_log,
                         gdn_dt_bias, gdn_norm_w) for i in range(n_even)]
    odd_p = [_prep_odd(i, lru_conv_w, lru_conv_b, lru_wa, lru_ba, lru_wi, lru_bi, lru_L)
             for i in range(n_odd)]
    mlp_w = (mlp_up.astype(BF16), mlp_down.astype(BF16))
    out_w = (w_out_even.astype(BF16), w_out_odd.astype(BF16))
    bp = x_prompt.shape[0]
    dt = x_prompt.dtype
    zeros = lambda *s: jnp.zeros(s, dt)
    shared = (_prep_in_proj(w_in_even, w_in_odd), out_w, mlp_w, even_p, odd_p, norm_mix, norm_mlp, norm_final)
    y_p, rwkv_p, shift_p, gdn_p, gconv_p, lru_p, lconv_p = _trunk(
        x_prompt,
        zeros(n_even, bp, RWKV_HEADS, RWKV_HD, RWKV_HD), zeros(n_even, bp, RWKV_PROJ),
        zeros(n_even, bp, GDN_HEADS, GDN_HD, GDN_HD), zeros(n_even, bp, CONV_W - 1, GDN_QKV),
        zeros(n_odd, bp, LRU_W), zeros(n_odd, bp, CONV_W - 1, LRU_W), *shared)
    y_s, rwkv_s, shift_s, gdn_s, gconv_s, lru_s, lconv_s = _trunk(
        x_sample, state_rwkv, state_rwkv_shift, state_gdn, state_gdn_conv, state_lru, state_lru_conv, *shared)
    return (y_p, y_s, rwkv_p, rwkv_s, shift_p, shift_s, gdn_p, gdn_s, gconv_p, gconv_s, lru_p, lru_s,
            lconv_p, lconv_s)
```

```python
import functools

import jax
import jax.numpy as jnp
from jax import lax
from jax.experimental import pallas as pl
from jax.experimental.pallas import tpu as pltpu

F32 = jnp.float32
BF16 = jnp.bfloat16

D_MODEL = 1024
DEPTH = 4
CONV_W = 4
NORM_EPS = 1e-6
RWKV_HEADS = 8
RWKV_HD = 64
RWKV_W = RWKV_HEADS * RWKV_HD
RWKV_PAIRS = RWKV_HEADS // 2
DECAY_LORA = 64
AAA_LORA = 64
GATE_LORA = 128
RWKV_PROJ = 3 * RWKV_W + DECAY_LORA + AAA_LORA + GATE_LORA
RWKV_GN_EPS = 64e-5
RWKV_DECAY_SCALE = 0.6065306597126334
GDN_HEADS = 4
GDN_HD = 128
GDN_W = GDN_HEADS * GDN_HD
GDN_QKV = 3 * GDN_W
GDN_BA_PAD = 128
LRU_W = D_MODEL
LRU_HEADS = 8
LRU_BW = LRU_W // LRU_HEADS
LRU_C = 8.0
D_FF = 4 * D_MODEL

SUBLANES = 8
LANES = 128
VMEM_LIMIT_BYTES = 56 * 1024 * 1024
MAX_SEQS_PER_STEP = 16


def _mm(a, b):
    return jnp.dot(a.astype(BF16), b.astype(BF16), preferred_element_type=F32)


def _mm_nt(a, b):
    return lax.dot_general(a.astype(BF16), b.astype(BF16), (((1,), (1,)), ((), ())),
                           preferred_element_type=F32)


def _mm_tn(a, b):
    return lax.dot_general(a.astype(BF16), b.astype(BF16), (((0,), (0,)), ((), ())),
                           preferred_element_type=F32)


def _split2(x):
    hi = x.astype(BF16)
    return hi, (x - hi.astype(F32)).astype(BF16)


def _mm_exact_lhs(m01, x):
    hi, lo = _split2(x)
    d = lambda t: jnp.dot(m01, t, preferred_element_type=F32)
    return d(hi) + d(lo)


def _mm_exact_rhs(x, m01):
    hi, lo = _split2(x)
    d = lambda t: jnp.dot(t, m01, preferred_element_type=F32)
    return d(hi) + d(lo)


def _mm3(a, b):
    ah, al = _split2(a)
    bh, bl = _split2(b)
    d = lambda x, y: jnp.dot(x, y, preferred_element_type=F32)
    return d(ah, bh) + d(ah, bl) + d(al, bh)


def _softplus(x):
    return jnp.maximum(x, 0.0) + jnp.log1p(jnp.exp(-jnp.abs(x)))


def _sigmoid(x):
    return jax.nn.sigmoid(x)


def _silu(x):
    return x * jax.nn.sigmoid(x)


def _gelu_tanh(x):
    return 0.5 * x * (1.0 + jnp.tanh(0.7978845608028654 * (x + 0.044715 * (x * x * x))))


def _block_masks(n, c):
    row = lax.broadcasted_iota(jnp.int32, (n, n), 0)
    col = lax.broadcasted_iota(jnp.int32, (n, n), 1)
    same = jnp.bitwise_xor(row, col) < c
    return same & (row >= col), same & (row > col), same, row == col


def _unit_lower_inverses(ms, eye, c):
    n = eye.shape[0]
    row = lax.broadcasted_iota(jnp.int32, (n, n), 0)
    col = lax.broadcasted_iota(jnp.int32, (n, n), 1)
    corner = lambda s: ((jnp.bitwise_xor(row, col) < 2 * s) & (jnp.bitwise_and(row, s) != 0)
                        & (jnp.bitwise_and(col, s) == 0))
    ps = [eye + jnp.where(corner(1), m, 0.0) for m in ms]
    s = 2
    while s < c:
        mask = corner(s)
        pb = [_mm(p, jnp.where(mask, m, 0.0)) for p, m in zip(ps, ms)]
        ps = [p + _mm(x, p) for p, x in zip(ps, pb)]
        s *= 2
    return ps


def _rows(x, s, c):
    return x[s * c:(s + 1) * c]


def _cat(xs, axis=0):
    return xs[0] if len(xs) == 1 else jnp.concatenate(xs, axis=axis)


def _const_spec(shape):
    nd = len(shape)
    return pl.BlockSpec(shape, lambda *_: (0,) * nd)


def _skip_first_ref(kern, _, *refs):
    return kern(*refs)


def _in_place_slot(kern, s_prev, out_index):
    return (functools.partial(_skip_first_ref, kern), [pl.BlockSpec(memory_space=pl.ANY)], [s_prev],
            {0: out_index})


def _causal_conv(x_ref, xp_ref, cw_ref, s, c):
    base = SUBLANES
    nb = CONV_W - 1
    xp_ref[s, base:base + c, :] = x_ref[s]
    y = cw_ref[0:1, :] * xp_ref[s, base - nb:base - nb + c, :]
    for j in range(1, CONV_W):
        y = y + cw_ref[j:j + 1, :] * xp_ref[s, base - nb + j:base - nb + j + c, :]
    xp_ref[s, base - nb:base, :] = xp_ref[s, base + c - nb:base + c, :]
    return y


def _norm_proj_kernel(x_ref, nw_ref, *refs, groups, col_chunk):
    n_w = len(groups)
    w_refs, o_refs = refs[:n_w], refs[n_w:]
    x = x_ref[...]
    var = jnp.mean(x * x, axis=-1, keepdims=True)
    xn = (x * lax.rsqrt(var + NORM_EPS) * nw_ref[...]).astype(BF16)
    o_iter = iter(o_refs)
    for w_ref, widths in zip(w_refs, groups):
        off = 0
        for width in widths:
            o_ref = next(o_iter)
            for c0 in range(0, width, col_chunk):
                cw = min(col_chunk, width - c0)
                o_ref[:, c0:c0 + cw] = jnp.dot(xn, w_ref[:, off + c0:off + c0 + cw],
                                               preferred_element_type=F32)
            off += width


def _norm_proj(x, nw, ws, layer, groups, tm):
    n = x.shape[0]
    assert n % tm == 0 and all(sum(gr) <= w.shape[2] for gr, w in zip(groups, ws))
    kern = functools.partial(_norm_proj_kernel, groups=tuple(tuple(gr) for gr in groups), col_chunk=512)
    widths = [wd for gr in groups for wd in gr]
    return pl.pallas_call(
        kern,
        grid=(n // tm,),
        in_specs=[
            pl.BlockSpec((tm, D_MODEL), lambda i: (i, 0)),
            _const_spec((1, D_MODEL)),
        ] + [pl.BlockSpec((None, D_MODEL, w.shape[2]), lambda i: (layer, 0, 0)) for w in ws],
        out_specs=[pl.BlockSpec((tm, wd), lambda i: (i, 0)) for wd in widths],
        out_shape=[jax.ShapeDtypeStruct((n, wd), F32) for wd in widths],
        compiler_params=pltpu.CompilerParams(
            dimension_semantics=("parallel",), vmem_limit_bytes=VMEM_LIMIT_BYTES),
        name="norm_proj",
    )(x, nw.reshape(1, D_MODEL), *ws)


def _mlp_kernel(*refs, n_mix, ff_chunk, final):
    h_ref = refs[0]
    y_refs = refs[1:1 + n_mix]
    wo_ref, nw_ref, up_ref, down_ref = refs[1 + n_mix:5 + n_mix]
    rest = refs[5 + n_mix:]
    if final:
        fnw_ref, o_ref, yf_ref = rest
    else:
        (o_ref,) = rest
    y = _cat([y_ref[...].astype(BF16) for y_ref in y_refs], axis=1)
    h = h_ref[...] + jnp.dot(y, wo_ref[...], preferred_element_type=F32)
    var = jnp.mean(h * h, axis=-1, keepdims=True)
    xn = (h * lax.rsqrt(var + NORM_EPS) * nw_ref[...]).astype(BF16)
    acc = h
    for c0 in range(0, D_FF, ff_chunk):
        u = jnp.dot(xn, up_ref[:, c0:c0 + ff_chunk], preferred_element_type=F32)
        r = jnp.square(jnp.maximum(u, 0.0)).astype(BF16)
        acc = acc + jnp.dot(r, down_ref[c0:c0 + ff_chunk, :], preferred_element_type=F32)
    o_ref[...] = acc
    if final:
        var2 = jnp.mean(acc * acc, axis=-1, keepdims=True)
        yf_ref[...] = acc * lax.rsqrt(var2 + NORM_EPS) * fnw_ref[...]


def _mix_out_mlp(h, ys, wo, wo_layer, nw, up, down, layer, final_nw, tm):
    n = h.shape[0]
    n_mix = len(ys)
    final = final_nw is not None
    kern = functools.partial(_mlp_kernel, n_mix=n_mix, ff_chunk=512, final=final)
    row = lambda wd: pl.BlockSpec((tm, wd), lambda i: (i, 0))
    pick = lambda w, l: pl.BlockSpec((None,) + w.shape[1:], lambda i: (l, 0, 0))
    in_specs = [row(D_MODEL)] + [row(y.shape[1]) for y in ys]
    in_specs += [pick(wo, wo_layer), _const_spec((1, D_MODEL)), pick(up, layer), pick(down, layer)]
    args = [h, *ys, wo, nw.reshape(1, D_MODEL), up, down]
    out_specs = [row(D_MODEL)]
    out_shape = [jax.ShapeDtypeStruct((n, D_MODEL), F32)]
    if final:
        in_specs.append(_const_spec((1, D_MODEL)))
        args.append(final_nw.reshape(1, D_MODEL))
        out_specs.append(row(D_MODEL))
        out_shape.append(jax.ShapeDtypeStruct((n, D_MODEL), F32))
    return pl.pallas_call(
        kern,
        grid=(n // tm,),
        in_specs=in_specs,
        out_specs=out_specs,
        out_shape=out_shape,
        compiler_params=pltpu.CompilerParams(
            dimension_semantics=("parallel",), vmem_limit_bytes=VMEM_LIMIT_BYTES),
        name="mix_out_mlp",
    )(*args)


def _rwkv_kernel(p_ref, prev_ref, s0_ref, mu_ref, w0_ref, w2_ref, a0_ref, a2_ref, g2_ref,
                 kk_ref, ka_ref, rk_ref, lnw_ref, lnb_ref, hsum_ref,
                 out_ref, s_out_ref, xp_ref, state_ref, *, c, g, gs, nc):
    ci = pl.program_id(1)
    base = SUBLANES
    gc = g * c

    hd = RWKV_HD

    @pl.when(ci == 0)
    def _():
        for s in range(g):
            xp_ref[s, base - 1:base, :] = prev_ref[s]
        state_ref[...] = jnp.zeros(state_ref.shape, F32)
        for s in range(g):
            for j in range(RWKV_PAIRS):
                state_ref[s, j, 0:hd, 0:hd] = s0_ref[s, 2 * j]
                state_ref[s, j, hd:2 * hd, hd:2 * hd] = s0_ref[s, 2 * j + 1]

    shifted = []
    for s in range(g):
        ps = p_ref[s]
        xp_ref[s, base:base + c, :] = ps
        shifted.append(xp_ref[s, base - 1:base - 1 + c, :])
        xp_ref[s, base - 1:base, :] = ps[c - 1:c, :]
    p = p_ref[...].reshape(gc, RWKV_PROJ)
    shifted = _cat(shifted)

    xs = p + mu_ref[...] * (shifted - p)
    r = xs[:, 0:RWKV_W]
    k = xs[:, RWKV_W:2 * RWKV_W]
    v = xs[:, 2 * RWKV_W:3 * RWKV_W]
    xwa = xs[:, 3 * RWKV_W:3 * RWKV_W + DECAY_LORA + AAA_LORA]
    xg = xs[:, 3 * RWKV_W + DECAY_LORA + AAA_LORA:]

    ld = -RWKV_DECAY_SCALE * _sigmoid(w0_ref[...] + _mm(jnp.tanh(xwa), w2_ref[...]))
    a = _sigmoid(a0_ref[...] + _mm(xwa, a2_ref[...]))
    gate = _mm(_sigmoid(xg), g2_ref[...])

    incl1, _, same1, _ = _block_masks(gc, c)
    sums = _mm_exact_lhs(jnp.concatenate([incl1, same1], axis=0).astype(BF16), ld)
    cum = sums[:gc]
    tot = sums[gc:]
    w_in = jnp.exp(cum)
    w_ex = jnp.exp(cum - ld)
    w_inv = jnp.exp(-cum)
    w_tail = jnp.exp(tot - cum)
    w_last = jnp.exp(tot)

    hsum = hsum_ref[...]

    def head_sums(x):
        return jnp.concatenate(
            [_mm_exact_rhs(x[:, j * LANES:(j + 1) * LANES], hsum) for j in range(RWKV_PAIRS)], axis=1)

    kkr = k * kk_ref[...]
    kk = kkr * lax.rsqrt(head_sums(kkr * kkr) + 1e-6)
    kp = k * (1.0 + (a - 1.0) * ka_ref[...])
    bv = kk * a

    ns = g // gs
    n = gs * c
    nh = RWKV_HEADS
    lo = lax.broadcasted_iota(jnp.int32, (n, LANES), 1) < RWKV_HD

    def units(x):
        out = []
        for st in range(ns):
            for j in range(RWKV_PAIRS):
                slab = x[st * n:(st + 1) * n, j * LANES:(j + 1) * LANES]
                out += [jnp.where(lo, slab, 0.0), jnp.where(lo, 0.0, slab)]
        return out

    a_t = units(-kk * w_ex)
    r_t = units(r * w_in)
    b_t = units(bv * w_inv)
    k_t = units(kp * w_inv)
    v_b = units(v)
    b_w = units(bv * w_tail)
    k_w = units(kp * w_tail)

    us_ = range(ns * nh)
    incl, strict, _, diag = _block_masks(n, c)
    eye = diag.astype(F32)
    ars = [jnp.concatenate([a_t[u], r_t[u]], axis=0) for u in us_]
    gbs = [_mm_nt(ars[u], b_t[u]) for u in us_]
    gks = [_mm_nt(ars[u], k_t[u]) for u in us_]
    m_ab = [jnp.where(strict, x[:n], 0.0) for x in gbs]
    m_ak = [jnp.where(strict, x[:n], 0.0) for x in gks]
    l_rb = [jnp.where(incl, x[n:], 0.0) for x in gbs]
    l_rk = [jnp.where(incl, x[n:], 0.0) for x in gks]
    t_inv = _unit_lower_inverses(m_ab, eye, c)

    sa = [[] for _ in us_]
    sr = [[] for _ in us_]
    for st in range(ns):
        for j in range(RWKV_PAIRS):
            u0 = st * nh + 2 * j
            for sl in range(gs):
                lhs = jnp.concatenate([_rows(x[u0 + i], sl, c) for x in (a_t, r_t) for i in range(2)], axis=0)
                ps = _mm_nt(lhs, state_ref[st * gs + sl, j])
                for i in range(2):
                    sa[u0 + i].append(_rows(ps, i, c))
                    sr[u0 + i].append(_rows(ps, 2 + i, c))
    sa = [_cat(x) for x in sa]
    sr = [_cat(x) for x in sr]

    mv = [_mm(m_ak[u], v_b[u]) for u in us_]
    us = [_mm3(t_inv[u], sa[u] + mv[u]) for u in us_]
    ys = [sr[u] + _mm(l_rb[u], us[u]) + _mm(l_rk[u], v_b[u]) for u in us_]
    y = _cat([jnp.concatenate([ys[st * nh + 2 * j] + ys[st * nh + 2 * j + 1] for j in range(RWKV_PAIRS)], axis=1)
              for st in range(ns)])

    for st in range(ns):
        for j in range(RWKV_PAIRS):
            u0 = st * nh + 2 * j
            for sl in range(gs):
                s = st * gs + sl
                uv = jnp.concatenate([_rows(x[u0 + i], sl, c) for x in (us, v_b) for i in range(2)], axis=0)
                bk = jnp.concatenate([_rows(x[u0 + i], sl, c) for x in (b_w, k_w) for i in range(2)], axis=0)
                decay = w_last[s * c:s * c + 1, j * LANES:(j + 1) * LANES]
                state_ref[s, j] = state_ref[s, j] * decay + _mm_tn(uv, bk)

    inv_hd = 1.0 / RWKV_HD
    sums = head_sums(jnp.concatenate([y, y * y, r * kp * rk_ref[...]], axis=0))
    mean = sums[:gc] * inv_hd
    var = sums[gc:2 * gc] * inv_hd - mean * mean
    yn = (y - mean) * lax.rsqrt(var + RWKV_GN_EPS) * lnw_ref[...] + lnb_ref[...]
    out_ref[...] = ((yn + sums[2 * gc:] * v) * gate).reshape(g, c, RWKV_W)

    @pl.when(ci == nc - 1)
    def _():
        for s in range(g):
            for j in range(RWKV_PAIRS):
                s_out_ref[s, 2 * j] = state_ref[s, j, 0:hd, 0:hd]
                s_out_ref[s, 2 * j + 1] = state_ref[s, j, hd:2 * hd, hd:2 * hd]


def _rwkv_mix(p, prev, s0, s_prev, layer, prm, c, g, gs):
    b, t, _ = p.shape
    nc = t // c
    assert nc * c == t and c % SUBLANES == 0 and b % g == 0
    kern = functools.partial(_rwkv_kernel, c=c, g=g, gs=gs, nc=nc)
    kern, prev_specs, prev_args, aliases = _in_place_slot(kern, s_prev, out_index=1)
    consts = [prm["mu"], prm["w0"], prm["w2"], prm["a0"], prm["a2"], prm["g2"], prm["k_k"], prm["k_a"],
              prm["r_k"], prm["ln_w"], prm["ln_b"], prm["hsum"]]
    st_block = (g, RWKV_HEADS, RWKV_HD, RWKV_HD)
    return pl.pallas_call(
        kern,
        grid=(b // g, nc),
        in_specs=prev_specs + [
            pl.BlockSpec((g, c, RWKV_PROJ), lambda i, j: (i, j, 0)),
            pl.BlockSpec((g, 1, RWKV_PROJ), lambda i, j: (i, 0, 0)),
            pl.BlockSpec((None,) + st_block, lambda i, j: (layer, i, 0, 0, 0)),
        ] + [_const_spec(x.shape) for x in consts],
        out_specs=[
            pl.BlockSpec((g, c, RWKV_W), lambda i, j: (i, j, 0)),
            pl.BlockSpec((None,) + st_block, lambda i, j: (layer, i, 0, 0, 0)),
        ],
        out_shape=[
            jax.ShapeDtypeStruct((b, t, RWKV_W), F32),
            jax.ShapeDtypeStruct(s0.shape, F32),
        ],
        input_output_aliases=aliases,
        scratch_shapes=[
            pltpu.VMEM((g, c + SUBLANES, RWKV_PROJ), F32),
            pltpu.VMEM((g, RWKV_PAIRS, LANES, LANES), F32),
        ],
        compiler_params=pltpu.CompilerParams(
            dimension_semantics=("parallel", "arbitrary"), vmem_limit_bytes=VMEM_LIMIT_BYTES),
        name="rwkv7_mix",
    )(*prev_args, p, prev.reshape(b, 1, RWKV_PROJ), s0, *consts)


def _gdn_kernel(qkv_ref, z_ref, ba_ref, buf_ref, s0_ref, cw_ref, alog_ref, dt_ref, nw_ref,
                out_ref, s_out_ref, xp_ref, state_ref, *, c, g, gs, nc):
    ci = pl.program_id(1)
    base = SUBLANES
    nb = CONV_W - 1
    gc = g * c

    @pl.when(ci == 0)
    def _():
        for s in range(g):
            xp_ref[s, base - nb:base, :] = buf_ref[s]
        state_ref[...] = s0_ref[...]

    qkv = _silu(_cat([_causal_conv(qkv_ref, xp_ref, cw_ref, s, c) for s in range(g)]))
    z = z_ref[...].reshape(gc, GDN_W)
    ba = ba_ref[...].reshape(gc, GDN_BA_PAD)
    beta_all = _sigmoid(ba)
    g_all = -jnp.exp(alog_ref[...]) * _softplus(ba + dt_ref[...])
    incl_all, strict_all, same_all, _ = _block_masks(gc, c)
    sums = _mm_exact_lhs(jnp.concatenate([incl_all, same_all], axis=0).astype(BF16), g_all)
    gc_all = sums[:gc]
    tot_all = sums[gc:]
    upper = (same_all & jnp.logical_not(strict_all)).astype(BF16)
    dn = (((0,), (0,)), ((), ()))
    gct = sum(lax.dot_general(t, upper, dn, preferred_element_type=F32) for t in _split2(g_all))

    ns = g // gs
    n = gs * c
    incl, strict, _, diag = _block_masks(n, c)
    eye = diag.astype(F32)
    stack = lambda x, st: x[st * n:(st + 1) * n]

    hs = range(GDN_HEADS)
    head = lambda x, part, h: x[:, part * GDN_W + h * GDN_HD:part * GDN_W + (h + 1) * GDN_HD]
    q = [head(qkv, 0, h) for h in hs]
    k = [head(qkv, 1, h) for h in hs]
    v = [head(qkv, 2, h) for h in hs]
    q = [x * lax.rsqrt(jnp.sum(x * x, axis=-1, keepdims=True) + 1e-6) * (GDN_HD ** -0.5) for x in q]
    k = [x * lax.rsqrt(jnp.sum(x * x, axis=-1, keepdims=True) + 1e-6) for x in k]
    beta = [beta_all[:, h:h + 1] for h in hs]
    gcol = [gc_all[:, GDN_HEADS + h:GDN_HEADS + h + 1] for h in hs]
    grow = [gct[GDN_HEADS + h:GDN_HEADS + h + 1, :] for h in hs]
    glast = [tot_all[:, GDN_HEADS + h:GDN_HEADS + h + 1] for h in hs]
    kb = [k[h] * beta[h] for h in hs]
    vb = [v[h] * beta[h] for h in hs]
    eg = [jnp.exp(gcol[h]) for h in hs]
    qe = [q[h] * eg[h] for h in hs]
    kbe = [kb[h] * eg[h] for h in hs]
    k_tail = [k[h] * jnp.exp(glast[h] - gcol[h]) for h in hs]

    units = [(st, h) for st in range(ns) for h in hs]
    decay = [jnp.where(incl, jnp.exp(jnp.where(incl, stack(gcol[h], st) - grow[h][:, st * n:(st + 1) * n], 0.0)),
                       0.0) for st, h in units]
    gq = [_mm_nt(jnp.concatenate([stack(q[h], st), stack(kb[h], st)], axis=0), stack(k[h], st))
          for st, h in units]
    qk = [jnp.where(incl, x[:n] * d, 0.0) for x, d in zip(gq, decay)]
    lmat = [jnp.where(strict, x[n:] * d, 0.0) for x, d in zip(gq, decay)]
    t_inv = _unit_lower_inverses([-x for x in lmat], eye, c)
    uw = [_mm3(t, jnp.concatenate([stack(vb[h], st), stack(kbe[h], st)], axis=1))
          for t, (st, h) in zip(t_inv, units)]

    q_s, w_s = [], []
    for u, (st, h) in enumerate(units):
        pieces = []
        for sl in range(gs):
            lhs = jnp.concatenate([_rows(stack(qe[h], st), sl, c), _rows(uw[u][:, GDN_HD:], sl, c)], axis=0)
            pieces.append(_mm(lhs, state_ref[st * gs + sl, h]))
        q_s.append(_cat([x[:c] for x in pieces]))
        w_s.append(_cat([x[c:] for x in pieces]))
    v_new = [uw[u][:, :GDN_HD] - w_s[u] for u in range(len(units))]
    o_u = [q_s[u] + _mm(qk[u], v_new[u]) for u in range(len(units))]
    for u, (st, h) in enumerate(units):
        for sl in range(gs):
            s = st * gs + sl
            decay_s = jnp.exp(glast[h][s * c:s * c + 1, :])
            state_ref[s, h] = (state_ref[s, h] * decay_s
                               + _mm_tn(_rows(stack(k_tail[h], st), sl, c), _rows(v_new[u], sl, c)))
    o = [_cat([o_u[st * GDN_HEADS + h] for st in range(ns)]) for h in hs]
    outs = []
    for h in hs:
        on = o[h] * lax.rsqrt(jnp.mean(o[h] * o[h], axis=-1, keepdims=True) + NORM_EPS) * nw_ref[...]
        outs.append(on * _silu(z[:, h * GDN_HD:(h + 1) * GDN_HD]))
    out_ref[...] = jnp.concatenate(outs, axis=1).reshape(g, c, GDN_W)

    @pl.when(ci == nc - 1)
    def _():
        s_out_ref[...] = state_ref[...]


def _gdn_mix(qkv, z, ba, buf, s0, s_prev, layer, prm, c, g, gs):
    b, t, _ = qkv.shape
    nc = t // c
    assert nc * c == t and c % SUBLANES == 0 and b % g == 0
    kern = functools.partial(_gdn_kernel, c=c, g=g, gs=gs, nc=nc)
    kern, prev_specs, prev_args, aliases = _in_place_slot(kern, s_prev, out_index=1)
    consts = [prm["conv_w"], prm["a_log"], prm["dt_bias"], prm["norm_w"]]
    st_block = (g, GDN_HEADS, GDN_HD, GDN_HD)
    return pl.pallas_call(
        kern,
        grid=(b // g, nc),
        in_specs=prev_specs + [
            pl.BlockSpec((g, c, GDN_QKV), lambda i, j: (i, j, 0)),
            pl.BlockSpec((g, c, GDN_W), lambda i, j: (i, j, 0)),
            pl.BlockSpec((g, c, GDN_BA_PAD), lambda i, j: (i, j, 0)),
            pl.BlockSpec((g, CONV_W - 1, GDN_QKV), lambda i, j: (i, 0, 0)),
            pl.BlockSpec((None,) + st_block, lambda i, j: (layer, i, 0, 0, 0)),
        ] + [_const_spec(x.shape) for x in consts],
        out_specs=[
            pl.BlockSpec((g, c, GDN_W), lambda i, j: (i, j, 0)),
            pl.BlockSpec((None,) + st_block, lambda i, j: (layer, i, 0, 0, 0)),
        ],
        out_shape=[
            jax.ShapeDtypeStruct((b, t, GDN_W), F32),
            jax.ShapeDtypeStruct(s0.shape, F32),
        ],
        input_output_aliases=aliases,
        scratch_shapes=[
            pltpu.VMEM((g, c + SUBLANES, GDN_QKV), F32),
            pltpu.VMEM(st_block, F32),
        ],
        compiler_params=pltpu.CompilerParams(
            dimension_semantics=("parallel", "arbitrary"), vmem_limit_bytes=VMEM_LIMIT_BYTES),
        name="gdn_mix",
    )(*prev_args, qkv, z, ba, buf, s0, *consts)


def _lru_kernel(gate_ref, xb_ref, buf_ref, h0_ref, cw_ref, cb_ref, wab_ref, bab_ref, l_ref,
                out_ref, h_out_ref, xp_ref, carry_ref, sa_ref, sb_ref, *, c, g, nc):
    ci = pl.program_id(1)
    base = SUBLANES
    nb = CONV_W - 1
    gc = g * c

    @pl.when(ci == 0)
    def _():
        for s in range(g):
            xp_ref[s, base - nb:base, :] = buf_ref[s]
        carry_ref[...] = h0_ref[...]

    xc = _cat([_causal_conv(xb_ref, xp_ref, cw_ref, s, c) for s in range(g)]) + cb_ref[...]

    for h in range(LRU_HEADS):
        sl = slice(h * LRU_BW, (h + 1) * LRU_BW)
        xh = xc[:, sl]
        ri = _mm(xh, wab_ref[h]) + bab_ref[h]
        r = _sigmoid(ri[:, :LRU_BW])
        i = _sigmoid(ri[:, LRU_BW:])
        log_a = -LRU_C * r * _softplus(-l_ref[:, sl])
        a = jnp.exp(log_a)
        mult = jnp.sqrt(-jnp.tanh(log_a) * (a * a + 1.0))
        sa_ref[:, sl] = a
        sb_ref[:, sl] = mult * i * xh

    tpos = jnp.bitwise_and(lax.broadcasted_iota(jnp.int32, (gc, 1), 0), SUBLANES - 1)
    a = sa_ref[...]
    b = sb_ref[...]

    def shift_in_group(x, d):
        x3 = x.reshape(gc // SUBLANES, SUBLANES, LRU_W)
        return pltpu.roll(x3, d, axis=1).reshape(gc, LRU_W)

    d = 1
    while d < SUBLANES:
        live = tpos >= d
        a_sh = jnp.where(live, shift_in_group(a, d), 1.0)
        b_sh = jnp.where(live, shift_in_group(b, d), 0.0)
        b = a * b_sh + b
        a = a * a_sh
        d *= 2
    groups = []
    for s in range(g):
        h = carry_ref[s]
        for r0 in range(s * c, (s + 1) * c, SUBLANES):
            hg = a[r0:r0 + SUBLANES] * h + b[r0:r0 + SUBLANES]
            groups.append(hg)
            h = hg[SUBLANES - 1:SUBLANES]
        carry_ref[s] = h
    hs = _cat(groups)
    out_ref[...] = (hs * _gelu_tanh(gate_ref[...].reshape(gc, LRU_W))).reshape(g, c, LRU_W)

    @pl.when(ci == nc - 1)
    def _():
        h_out_ref[...] = carry_ref[...]


def _lru_mix(gate, xb, buf, h0, prm, c, g):
    b, t, _ = xb.shape
    nc = t // c
    assert nc * c == t and c % SUBLANES == 0 and b % g == 0
    kern = functools.partial(_lru_kernel, c=c, g=g, nc=nc)
    consts = [prm["conv_w"], prm["conv_b"], prm["wab"], prm["bab"], prm["l"]]
    seq = lambda rows: pl.BlockSpec((g, rows, LRU_W), lambda i, j: (i, 0, 0))
    chunk = pl.BlockSpec((g, c, LRU_W), lambda i, j: (i, j, 0))
    y, h_last = pl.pallas_call(
        kern,
        grid=(b // g, nc),
        in_specs=[chunk, chunk, seq(CONV_W - 1), seq(1)] + [_const_spec(x.shape) for x in consts],
        out_specs=[chunk, seq(1)],
        out_shape=[
            jax.ShapeDtypeStruct((b, t, LRU_W), F32),
            jax.ShapeDtypeStruct((b, 1, LRU_W), F32),
        ],
        scratch_shapes=[
            pltpu.VMEM((g, c + SUBLANES, LRU_W), F32),
            pltpu.VMEM((g, 1, LRU_W), F32),
            pltpu.VMEM((g * c, LRU_W), F32),
            pltpu.VMEM((g * c, LRU_W), F32),
        ],
        compiler_params=pltpu.CompilerParams(
            dimension_semantics=("parallel", "arbitrary"), vmem_limit_bytes=VMEM_LIMIT_BYTES),
        name="lru_mix",
    )(gate, xb, buf, h0.reshape(b, 1, LRU_W), *consts)
    return y, h_last.reshape(b, LRU_W)


def _prep_even(i, rwkv_mu, rwkv_w0, rwkv_w2, rwkv_a0, rwkv_a2, rwkv_g2, rwkv_k_k,
               rwkv_k_a, rwkv_r_k, rwkv_ln_w, rwkv_ln_b, gdn_conv_w, gdn_A_log, gdn_dt_bias, gdn_norm_w):
    n_ba = 2 * GDN_HEADS
    row = lambda x: x.reshape(1, -1)
    lane = jnp.arange(LANES) // RWKV_HD
    lora = DECAY_LORA + AAA_LORA
    rwkv = dict(
        mu=row(rwkv_mu[i]), w0=row(rwkv_w0[i]), a0=row(rwkv_a0[i]),
        w2=jnp.pad(rwkv_w2[i], ((0, lora - DECAY_LORA), (0, 0))).astype(BF16),
        a2=jnp.pad(rwkv_a2[i], ((DECAY_LORA, 0), (0, 0))).astype(BF16),
        g2=rwkv_g2[i].astype(BF16),
        k_k=row(rwkv_k_k[i]), k_a=row(rwkv_k_a[i]), r_k=row(rwkv_r_k[i]),
        ln_w=row(rwkv_ln_w[i]), ln_b=row(rwkv_ln_b[i]),
        hsum=(lane[:, None] == lane[None, :]).astype(BF16),
    )
    pad_heads = lambda x: jnp.pad(x, (GDN_HEADS, GDN_BA_PAD - n_ba)).reshape(1, GDN_BA_PAD)
    gdn = dict(conv_w=gdn_conv_w[i], a_log=pad_heads(gdn_A_log[i]), dt_bias=pad_heads(gdn_dt_bias[i]),
               norm_w=row(gdn_norm_w[i]))
    return rwkv, gdn


def _prep_odd(i, lru_conv_w, lru_conv_b, lru_wa, lru_ba, lru_wi, lru_bi, lru_L):
    lru = dict(
        conv_w=lru_conv_w[i], conv_b=lru_conv_b[i].reshape(1, LRU_W),
        wab=jnp.concatenate([lru_wa[i], lru_wi[i]], axis=-1).astype(BF16),
        bab=jnp.concatenate([lru_ba[i], lru_bi[i]], axis=-1).reshape(LRU_HEADS, 1, 2 * LRU_BW),
        l=lru_L[i].reshape(1, LRU_W),
    )
    return lru


def _prep_in_proj(w_in_even, w_in_odd):
    main = RWKV_PROJ + GDN_QKV + GDN_W
    n_ba = 2 * GDN_HEADS
    even_ba = jnp.pad(w_in_even[:, :, main:], ((0, 0), (0, 0), (0, GDN_BA_PAD - n_ba)))
    return w_in_even.astype(BF16), even_ba.astype(BF16), w_in_odd.astype(BF16)


def _row_tile(n):
    return 512 if n % 512 == 0 else 256


def _seq_group(b, c, rows):
    g = max(1, min(b, rows // c))
    while b % g:
        g -= 1
    return g


def _trunk(x, st_rwkv, st_shift, st_gdn, st_gconv, st_lru, st_lconv, in_w, out_w, mlp_w, even_p, odd_p,
           norm_mix, norm_mlp, norm_final):
    b, t, _ = x.shape
    n = b * t
    tm = _row_tile(n)
    c_even = min(64, t)
    gs_even = _seq_group(b, c_even, 128)
    fits = lambda k: k * gs_even <= MAX_SEQS_PER_STEP and b % (k * gs_even) == 0
    g_rwkv = gs_even * max(k for k in (1, 2) if fits(k))
    g_gdn = gs_even * max(k for k in (1, 2, 4) if fits(k))
    c_odd = min(256, t)
    g_odd = _seq_group(b, c_odd, 256)
    h = x.reshape(n, D_MODEL)
    n_shift, n_gconv, n_lru, n_lconv = [], [], [], []
    n_rwkv = jnp.zeros_like(st_rwkv)
    n_gdn = jnp.zeros_like(st_gdn)
    y_final = None
    for l in range(DEPTH):
        i = l // 2
        up, down = mlp_w
        wo = out_w[l % 2]
        final_nw = norm_final if l == DEPTH - 1 else None
        if l % 2 == 0:
            rwkv, gdn = even_p[i]
            p, qkv, z, ba = _norm_proj(h, norm_mix[l], in_w[:2], i,
                                       ((RWKV_PROJ, GDN_QKV, GDN_W), (GDN_BA_PAD,)), tm)
            p = p.reshape(b, t, RWKV_PROJ)
            qkv = qkv.reshape(b, t, GDN_QKV)
            ya, n_rwkv = _rwkv_mix(p, st_shift[i], st_rwkv, n_rwkv, i, rwkv, c_even, g_rwkv, gs_even)
            yb, n_gdn = _gdn_mix(qkv, z.reshape(b, t, GDN_W), ba.reshape(b, t, GDN_BA_PAD), st_gconv[i],
                                 st_gdn, n_gdn, i, gdn, c_even, g_gdn, gs_even)
            n_shift.append(p[:, -1])
            n_gconv.append(qkv[:, t - (CONV_W - 1):])
            ys = (ya.reshape(n, RWKV_W), yb.reshape(n, GDN_W))
        else:
            lru = odd_p[i]
            gate, xb = _norm_proj(h, norm_mix[l], in_w[2:], i, ((LRU_W, LRU_W),), tm)
            xb = xb.reshape(b, t, LRU_W)
            yc, h_last = _lru_mix(gate.reshape(b, t, LRU_W), xb, st_lconv[i], st_lru[i], lru, c_odd, g_odd)
            n_lru.append(h_last)
            n_lconv.append(xb[:, t - (CONV_W - 1):])
            ys = (yc.reshape(n, LRU_W),)
        outs = _mix_out_mlp(h, ys, wo, i, norm_mlp[l], up, down, l, final_nw, tm)
        h = outs[0]
        if final_nw is not None:
            y_final = outs[1]
    return (y_final.reshape(b, t, D_MODEL), n_rwkv, jnp.stack(n_shift), n_gdn,
            jnp.stack(n_gconv), jnp.stack(n_lru), jnp.stack(n_lconv))


def kernel(x_prompt, x_sample, state_rwkv, state_rwkv_shift, state_gdn, state_gdn_conv, state_lru, state_lru_conv, norm_mix, norm_mlp, norm_final, w_in_even, w_out_even, rwkv_mu, rwkv_w0, rwkv_w2, rwkv_a0, rwkv_a2, rwkv_g2, rwkv_k_k, rwkv_k_a, rwkv_r_k, rwkv_ln_w, rwkv_ln_b, gdn_conv_w, gdn_A_log, gdn_dt_bias, gdn_norm_w, w_in_odd, w_out_odd, lru_conv_w, lru_conv_b, lru_wa, lru_ba, lru_wi, lru_bi, lru_L, mlp_up, mlp_down):
    n_even = (DEPTH + 1) // 2
    n_odd = DEPTH // 2
    even_p = [_prep_even(i, rwkv_mu, rwkv_w0, rwkv_w2, rwkv_a0, rwkv_a2, rwkv_g2,
                         rwkv_k_k, rwkv_k_a, rwkv_r_k, rwkv_ln_w, rwkv_ln_b, gdn_conv_w, gdn_A_log,
                         gdn_dt_bias, gdn_norm_w) for i in range(n_even)]
    odd_p = [_prep_odd(i, lru_conv_w, lru_conv_b, lru_wa, lru_ba, lru_wi, lru_bi, lru_L)
             for i in range(n_odd)]
    mlp_w = (mlp_up.astype(BF16), mlp_down.astype(BF16))
    out_w = (w_out_even.astype(BF16), w_out_odd.astype(BF16))
    bp = x_prompt.shape[0]
    dt = x_prompt.dtype
    zeros = lambda *s: jnp.zeros(s, dt)
    shared = (_prep_in_proj(w_in_even, w_in_odd), out_w, mlp_w, even_p, odd_p, norm_mix, norm_mlp, norm_final)
    y_p, rwkv_p, shift_p, gdn_p, gconv_p, lru_p, lconv_p = _trunk(
        x_prompt,
        zeros(n_even, bp, RWKV_HEADS, RWKV_HD, RWKV_HD), zeros(n_even, bp, RWKV_PROJ),
        zeros(n_even, bp, GDN_HEADS, GDN_HD, GDN_HD), zeros(n_even, bp, CONV_W - 1, GDN_QKV),
        zeros(n_odd, bp, LRU_W), zeros(n_odd, bp, CONV_W - 1, LRU_W), *shared)
    y_s, rwkv_s, shift_s, gdn_s, gconv_s, lru_s, lconv_s = _trunk(
        x_sample, state_rwkv, state_rwkv_shift, state_gdn, state_gdn_conv, state_lru, state_lru_conv, *shared)
    return (y_p, y_s, rwkv_p, rwkv_s, shift_p, shift_s, gdn_p, gdn_s, gconv_p, gconv_s, lru_p, lru_s,
            lconv_p, lconv_s)
```
